```python
import math
import jax, jax.numpy as jnp
from jax import lax
import numpy as np

D_MODEL = 1024
BATCH = 8
SEQ = 2048
DEPTH = 4
DEC_BATCH = 128
DEC_SEQ = 8
PAST_LEN = 16384
PAGE_SIZE = 128

A_HEADS = 4
A_KEY = 128
A_VAL = 128
B_HEADS = 4
B_QK = 64
B_VAL = 128
C_HEADS = 4
C_KEY = 128
C_VAL = 128
CONV_W = 4
CONV_CH = C_HEADS * (2 * C_KEY + C_VAL)
N_BRANCH = 3
BRANCH_W = 512
FFN_HIDDEN = -(-(8 * D_MODEL) // (3 * 256)) * 256
CHUNK = 64
ALPHA = (2 * DEPTH) ** 0.25
BETA_INIT = (8 * DEPTH) ** -0.25
LN_EPS = 1e-5
NORM_EPS = 1e-6
NEG_BIG = -1e30

SPLITS = (A_HEADS * A_KEY, A_HEADS * A_KEY, A_HEADS * A_VAL, A_HEADS * A_VAL,
          B_HEADS * B_QK, B_HEADS * B_QK, B_HEADS * B_VAL, B_HEADS * B_VAL, B_HEADS, B_HEADS,
          CONV_CH, C_HEADS * C_VAL, C_HEADS, C_HEADS,
          N_BRANCH * D_MODEL)
N_IN = sum(SPLITS)
SPLIT_IDX = tuple(int(v) for v in np.cumsum(SPLITS)[:-1])

kernel_name = 'hybrid_hgrn2_mlstm_gdn_step'


def _layer_norm(x, g, b):
    mu = jnp.mean(x, -1, keepdims=True)
    xc = x - mu
    var = jnp.mean(xc * xc, -1, keepdims=True)
    return xc * lax.rsqrt(var + LN_EPS) * g.astype(jnp.float32) + b.astype(jnp.float32)


def _heads(t, h):
    B, L, _ = t.shape
    return t.reshape(B, L, h, -1).transpose(0, 2, 1, 3)


def _head_rms_merge(o, g):
    o = o * lax.rsqrt(jnp.mean(o * o, -1, keepdims=True) + NORM_EPS)
    B, H, L, d = o.shape
    return o.transpose(0, 2, 1, 3).reshape(B, L, H * d) * g.astype(jnp.float32)


def _l2norm(t):
    return t * lax.rsqrt(jnp.sum(t * t, -1, keepdims=True) + NORM_EPS)


def _masked_exp(mask, t):
    return jnp.where(mask, jnp.exp(jnp.where(mask, t, 0.0)), 0.0)


def _chunk(t, c):
    B, H, L = t.shape[:3]
    t = t.reshape((B, H, L // c, c) + t.shape[3:])
    return jnp.moveaxis(t, 2, 0)


def _unchunk(o):
    n, B, H, c = o.shape[:4]
    o = jnp.moveaxis(o, 0, 2)
    return o.reshape((B, H, n * c) + o.shape[4:])


def _hgrn2_scan(q, k, v, logf, S0):
    c = math.gcd(q.shape[2], CHUNK)
    incl = jnp.tril(jnp.ones((c, c), bool))

    def step(S, blk):
        qc, kc, vc, fc = blk
        b = jnp.cumsum(fc, axis=2)
        diff = b[:, :, :, None, :] - b[:, :, None, :, :]
        dec = _masked_exp(incl[:, :, None], diff)
        att = jnp.einsum('bhtk,bhsk,bhtsk->bhts', qc, kc, dec)
        o = jnp.einsum('bhtk,bhkv->bhtv', qc * jnp.exp(b), S) + jnp.einsum('bhts,bhsv->bhtv', att, vc)
        bl = b[:, :, -1, :]
        S_new = jnp.exp(bl)[..., None] * S + jnp.einsum('bhsk,bhsv->bhkv', kc * jnp.exp(bl[:, :, None, :] - b), vc)
        return S_new, o

    S, o = lax.scan(step, S0, (_chunk(q, c), _chunk(k, c), _chunk(v, c), _chunk(logf, c)))
    return _unchunk(o), S


def _mlstm_scan(q, k, v, ig, lf, C0, n0, m0):
    c = math.gcd(q.shape[2], CHUNK)
    incl = jnp.tril(jnp.ones((c, c), bool))

    def step(carry, blk):
        Cm, n, m = carry
        qc, kc, vc, ic, fc = blk
        b = jnp.cumsum(fc, axis=-1)
        logd = jnp.where(incl, b[..., :, None] - b[..., None, :] + ic[..., None, :], NEG_BIG)
        m_inter = b + m[..., None]
        m_t = jnp.maximum(m_inter, jnp.max(logd, -1))
        w_inter = jnp.exp(m_inter - m_t)
        sc = jnp.einsum('bhtk,bhsk->bhts', qc, kc) * _masked_exp(incl, logd - m_t[..., None])
        num = w_inter[..., None] * jnp.einsum('bhvk,bhtk->bhtv', Cm, qc) + jnp.einsum('bhts,bhsv->bhtv', sc, vc)
        den = w_inter * jnp.einsum('bhk,bhtk->bht', n, qc) + jnp.sum(sc, -1)
        h = num / jnp.maximum(jnp.abs(den), jnp.exp(-m_t))[..., None]
        g = b[..., -1:] - b + ic
        m_new = jnp.maximum(b[..., -1] + m, jnp.max(g, -1))
        w_old = jnp.exp(b[..., -1] + m - m_new)
        wk = jnp.exp(g - m_new[..., None])
        C_new = w_old[..., None, None] * Cm + jnp.einsum('bhs,bhsv,bhsk->bhvk', wk, vc, kc)
        n_new = w_old[..., None] * n + jnp.einsum('bhs,bhsk->bhk', wk, kc)
        return (C_new, n_new, m_new), h

    (Cm, n, m), o = lax.scan(step, (C0, n0, m0),
                             (_chunk(q, c), _chunk(k, c), _chunk(v, c), _chunk(ig, c), _chunk(lf, c)))
    return _unchunk(o), Cm, n, m


def _gdn_scan(q, k, v, beta, g, S0):
    c = math.gcd(q.shape[2], CHUNK)
    strict = jnp.tril(jnp.ones((c, c), bool), -1)
    incl = jnp.tril(jnp.ones((c, c), bool))
    eye = jnp.eye(c, dtype=jnp.float32)
    dv = v.shape[-1]

    def step(S, blk):
        qc, kc, vc, bc, gc = blk
        b = jnp.cumsum(gc, axis=-1)
        diff = b[..., :, None] - b[..., None, :]
        a = bc[..., :, None] * jnp.einsum('bhtk,bhsk->bhts', kc, kc) * _masked_exp(strict, diff)
        rhs = jnp.concatenate([vc * bc[..., None], kc * (bc * jnp.exp(b))[..., None]], axis=-1)
        sol = lax.linalg.triangular_solve(eye + a, rhs, left_side=True, lower=True, unit_diagonal=True)
        v_new = sol[..., :dv] - jnp.einsum('bhtk,bhkv->bhtv', sol[..., dv:], S)
        att = jnp.einsum('bhtk,bhsk->bhts', qc, kc) * _masked_exp(incl, diff)
        o = jnp.einsum('bhtk,bhkv->bhtv', qc * jnp.exp(b)[..., None], S) + jnp.einsum('bhts,bhsv->bhtv', att, v_new)
        bl = b[..., -1]
        S_new = jnp.exp(bl)[..., None, None] * S + jnp.einsum('bhsk,bhsv->bhkv', kc * jnp.exp(bl[..., None] - b)[..., None], v_new)
        return S_new, o

    S, o = lax.scan(step, S0, (_chunk(q, c), _chunk(k, c), _chunk(v, c), _chunk(beta, c), _chunk(g, c)))
    return _unchunk(o), S


def _causal_conv(xc, buf, w):
    L = xc.shape[1]
    xp = jnp.concatenate([buf, xc], axis=1)
    out = xp[:, 0:L] * w[0]
    for j in range(1, CONV_W):
        out = out + xp[:, j:j + L] * w[j]
    return out, xp[:, L:]


def _trunk(x, st_a, st_c, st_n, st_m, st_g, st_conv,
           w_in, lb_logits, a_norm_g, b_mi, b_mf, b_norm_g, conv_w, a_log, dt_bias, c_norm_g,
           w_branch, w_out, ln1_g, ln1_b, w_ffn_in, w_ffn_out, ln2_g, ln2_b):
    f32 = jnp.float32
    dt = x.dtype
    B, L, _ = x.shape
    lb_w = jax.nn.softmax(lb_logits.astype(f32), axis=0)
    lb_all = jnp.cumsum(lb_w, axis=0) - lb_w[0]
    na, nc, nn, nm, ng, nv = [], [], [], [], [], []
    h = x
    for l in range(DEPTH):
        proj = jnp.einsum('bld,dn->bln', h, w_in[l])
        (aq, af, ai, ag, bq, bk, bv, bo, bi, bf, cx, cg, cb, ca, mg) = jnp.split(proj, SPLIT_IDX, axis=-1)

        lb = lb_all[l]
        af32 = af.astype(f32)
        a_logf = jnp.log(lb + (1.0 - lb) * jax.nn.sigmoid(af32))
        a_k = (1.0 - lb) * jax.nn.sigmoid(-af32)
        a_o, a_S = _hgrn2_scan(_heads(jax.nn.silu(aq.astype(f32)), A_HEADS), _heads(a_k, A_HEADS),
                               _heads(ai.astype(f32), A_HEADS), _heads(a_logf, A_HEADS), st_a[l].astype(f32))
        y_a = _head_rms_merge(a_o, a_norm_g[l]) * jax.nn.silu(ag.astype(f32))

        b_ig = (bi.astype(f32) + b_mi[l].astype(f32)).transpose(0, 2, 1)
        b_lf = jax.nn.log_sigmoid(bf.astype(f32) + b_mf[l].astype(f32)).transpose(0, 2, 1)
        b_o, b_C, b_n, b_m = _mlstm_scan(_heads(bq.astype(f32), B_HEADS) * B_QK ** -0.5,
                                         _heads(bk.astype(f32), B_HEADS), _heads(bv.astype(f32), B_HEADS),
                                         b_ig, b_lf, st_c[l].astype(f32), st_n[l].astype(f32), st_m[l].astype(f32))
        y_b = _head_rms_merge(b_o, b_norm_g[l]) * jax.nn.sigmoid(bo.astype(f32))

        cconv, cbuf = _causal_conv(cx.astype(f32), st_conv[l].astype(f32), conv_w[l].astype(f32))
        cconv = jax.nn.silu(cconv)
        cq, ck, cv = jnp.split(cconv, [C_HEADS * C_KEY, 2 * C_HEADS * C_KEY], axis=-1)
        cq = _l2norm(_heads(cq, C_HEADS)) * C_KEY ** -0.5
        ck = _l2norm(_heads(ck, C_HEADS))
        c_beta = jax.nn.sigmoid(cb.astype(f32)).transpose(0, 2, 1)
        c_g = (-jnp.exp(a_log[l].astype(f32)) * jax.nn.softplus(ca.astype(f32) + dt_bias[l].astype(f32))).transpose(0, 2, 1)
        c_o, c_S = _gdn_scan(cq, ck, _heads(cv, C_HEADS), c_beta, c_g, st_g[l].astype(f32))
        y_c = _head_rms_merge(c_o, c_norm_g[l]) * jax.nn.silu(cg.astype(f32))

        ys = jnp.stack([y_a, y_b, y_c], axis=2).astype(dt)
        z = jnp.einsum('blnc,ncd->blnd', ys, w_branch[l])
        gates = jax.nn.sigmoid(mg.reshape(B, L, N_BRANCH, D_MODEL).astype(f32))
        merged = jnp.sum(gates * z.astype(f32), axis=2).astype(dt)
        mix = merged @ w_out[l]
        h = _layer_norm(ALPHA * h.astype(f32) + mix.astype(f32), ln1_g[l], ln1_b[l]).astype(dt)

        gate, up = jnp.split(h @ w_ffn_in[l], 2, axis=-1)
        ff = (jax.nn.silu(gate) * up) @ w_ffn_out[l]
        h = _layer_norm(ALPHA * h.astype(f32) + ff.astype(f32), ln2_g[l], ln2_b[l]).astype(dt)

        na.append(a_S); nc.append(b_C); nn.append(b_n); nm.append(b_m); ng.append(c_S); nv.append(cbuf)
    return (h, jnp.stack(na), jnp.stack(nc), jnp.stack(nn), jnp.stack(nm), jnp.stack(ng), jnp.stack(nv))


def setup_inputs(seed: int = 0) -> dict:
    key = jax.random.key(seed)
    ks = jax.random.split(key, 32)
    f32 = jnp.float32

    def nrm(k, shape, s):
        return s * jax.random.normal(k, shape, f32)

    dt0 = jnp.exp(jax.random.uniform(ks[14], (DEPTH, C_HEADS), f32, math.log(1e-3), math.log(1e-1)))
    return {
        'x_prompt': nrm(ks[0], (BATCH, SEQ, D_MODEL), 1.0),
        'x_sample': nrm(ks[1], (DEC_BATCH, DEC_SEQ, D_MODEL), 1.0),
        'state_hgrn': nrm(ks[2], (DEPTH, DEC_BATCH, A_HEADS, A_KEY, A_VAL), 0.5),
        'state_mlstm_c': nrm(ks[3], (DEPTH, DEC_BATCH, B_HEADS, B_VAL, B_QK), 0.3),
        'state_mlstm_n': nrm(ks[4], (DEPTH, DEC_BATCH, B_HEADS, B_QK), 0.3),
        'state_mlstm_m': 1.0 + nrm(ks[5], (DEPTH, DEC_BATCH, B_HEADS), 0.5),
        'state_gdn': nrm(ks[6], (DEPTH, DEC_BATCH, C_HEADS, C_KEY, C_VAL), 0.2),
        'state_gdn_conv': nrm(ks[7], (DEPTH, DEC_BATCH, CONV_W - 1, CONV_CH), 1.0),
        'w_in': nrm(ks[8], (DEPTH, D_MODEL, N_IN), D_MODEL ** -0.5),
        'lb_logits': nrm(ks[9], (DEPTH, A_HEADS * A_KEY), 0.5),
        'a_norm_g': 1.0 + nrm(ks[10], (DEPTH, A_HEADS * A_VAL), 0.02),
        'b_mi': nrm(ks[11], (DEPTH, B_HEADS), 0.1),
        'b_mf': jnp.linspace(3.0, 6.0, B_HEADS, dtype=f32)[None, :] + nrm(ks[12], (DEPTH, B_HEADS), 0.1),
        'b_norm_g': 1.0 + nrm(ks[13], (DEPTH, B_HEADS * B_VAL), 0.02),
        'conv_w': nrm(ks[15], (DEPTH, CONV_W, CONV_CH), CONV_W ** -0.5),
        'a_log': jnp.log(jax.random.uniform(ks[16], (DEPTH, C_HEADS), f32, 1.0, 16.0)),
        'dt_bias': dt0 + jnp.log(-jnp.expm1(-dt0)),
        'c_norm_g': 1.0 + nrm(ks[17], (DEPTH, C_HEADS * C_VAL), 0.02),
        'w_branch': nrm(ks[18], (DEPTH, N_BRANCH, BRANCH_W, D_MODEL), BRANCH_W ** -0.5),
        'w_out': nrm(ks[19], (DEPTH, D_MODEL, D_MODEL), BETA_INIT * D_MODEL ** -0.5),
        'ln1_g': 1.0 + nrm(ks[20], (DEPTH, D_MODEL), 0.02),
        'ln1_b': nrm(ks[21], (DEPTH, D_MODEL), 0.02),
        'w_ffn_in': nrm(ks[22], (DEPTH, D_MODEL, 2 * FFN_HIDDEN), D_MODEL ** -0.5),
        'w_ffn_out': nrm(ks[23], (DEPTH, FFN_HIDDEN, D_MODEL), BETA_INIT * FFN_HIDDEN ** -0.5),
        'ln2_g': 1.0 + nrm(ks[24], (DEPTH, D_MODEL), 0.02),
        'ln2_b': nrm(ks[25], (DEPTH, D_MODEL), 0.02),
    }


def reference(x_prompt, x_sample, state_hgrn, state_mlstm_c, state_mlstm_n, state_mlstm_m, state_gdn,
              state_gdn_conv, w_in, lb_logits, a_norm_g, b_mi, b_mf, b_norm_g, conv_w, a_log, dt_bias,
              c_norm_g, w_branch, w_out, ln1_g, ln1_b, w_ffn_in, w_ffn_out, ln2_g, ln2_b):
    weights = (w_in, lb_logits, a_norm_g, b_mi, b_mf, b_norm_g, conv_w, a_log, dt_bias, c_norm_g,
               w_branch, w_out, ln1_g, ln1_b, w_ffn_in, w_ffn_out, ln2_g, ln2_b)
    f32 = jnp.float32
    Bp = x_prompt.shape[0]
    y_p, pa, pc, pn, pm, pg, pv = _trunk(
        x_prompt,
        jnp.zeros((DEPTH, Bp, A_HEADS, A_KEY, A_VAL), f32),
        jnp.zeros((DEPTH, Bp, B_HEADS, B_VAL, B_QK), f32),
        jnp.zeros((DEPTH, Bp, B_HEADS, B_QK), f32),
        jnp.zeros((DEPTH, Bp, B_HEADS), f32),
        jnp.zeros((DEPTH, Bp, C_HEADS, C_KEY, C_VAL), f32),
        jnp.zeros((DEPTH, Bp, CONV_W - 1, CONV_CH), f32),
        *weights)
    y_s, sa, sc, sn, sm, sg, sv = _trunk(
        x_sample, state_hgrn, state_mlstm_c, state_mlstm_n, state_mlstm_m, state_gdn, state_gdn_conv, *weights)
    return (y_p, y_s,
            pa.astype(state_hgrn.dtype), sa.astype(state_hgrn.dtype),
            pc.astype(state_mlstm_c.dtype), sc.astype(state_mlstm_c.dtype),
            pn.astype(state_mlstm_n.dtype), sn.astype(state_mlstm_n.dtype),
            pm.astype(state_mlstm_m.dtype), sm.astype(state_mlstm_m.dtype),
            pg.astype(state_gdn.dtype), sg.astype(state_gdn.dtype),
            pv.astype(state_gdn_conv.dtype), sv.astype(state_gdn_conv.dtype))
```

```python
import functools
import math

import jax
import jax.numpy as jnp
from jax import lax
from jax.experimental import pallas as pl
from jax.experimental.pallas import tpu as pltpu

F32 = jnp.float32
BF16 = jnp.bfloat16

D_MODEL = 1024
DEPTH = 4
A_HEADS, A_KEY, A_VAL = 4, 128, 128
B_HEADS, B_QK, B_VAL = 4, 64, 128
C_HEADS, C_KEY, C_VAL = 4, 128, 128
CONV_W = 4
CONV_CH = C_HEADS * (2 * C_KEY + C_VAL)
N_BRANCH = 3
BRANCH_W = 512
FFN_HIDDEN = 2816
CHUNK = 64
ALPHA = (2 * DEPTH) ** 0.25
LN_EPS = 1e-5
NORM_EPS = 1e-6
NEG_BIG = -1e30

COL_A = 0
COL_C = 2048
COL_MG = 4096
COL_B = 7168
COL_S = 8704
N_PROJ = 8832
PROJ_TN = N_PROJ // 3

VMEM_LIMIT = 56 * 1024 * 1024


def _reorder_w_in(w_in):
    a = w_in[..., 0:2048]
    b = w_in[..., 2048:3584]
    s1 = w_in[..., 3584:3592]
    c = w_in[..., 3592:5640]
    s2 = w_in[..., 5640:5648]
    mg = w_in[..., 5648:8720]
    pad = jnp.zeros(w_in.shape[:-1] + (N_PROJ - COL_S - 16,), w_in.dtype)
    return jnp.concatenate([a, c, mg, b, s1, s2, pad], axis=-1).astype(BF16)


def _proj_body(x_ref, w_ref, o_ref):
    o_ref[...] = jnp.dot(x_ref[...].astype(BF16), w_ref[...], preferred_element_type=F32)


def _proj(h, w):
    T = h.shape[0]
    tm = 1024
    return pl.pallas_call(
        _proj_body,
        grid=(T // tm, N_PROJ // PROJ_TN),
        in_specs=[pl.BlockSpec((tm, D_MODEL), lambda i, j: (i, 0)),
                  pl.BlockSpec((D_MODEL, PROJ_TN), lambda i, j: (0, j))],
        out_specs=pl.BlockSpec((tm, PROJ_TN), lambda i, j: (i, j)),
        out_shape=jax.ShapeDtypeStruct((T, N_PROJ), F32),
        compiler_params=pltpu.CompilerParams(
            dimension_semantics=("arbitrary", "arbitrary"), vmem_limit_bytes=VMEM_LIMIT),
        name="in_proj",
    )(h, w)


def _ln(x, g, b):
    mu = jnp.mean(x, -1, keepdims=True)
    xc = x - mu
    var = jnp.mean(xc * xc, -1, keepdims=True)
    return xc * lax.rsqrt(var + LN_EPS) * g + b


FFN_HC = 256


def _post_body(ys_ref, mg0_ref, mg1_ref, mg2_ref, h_ref, wb_ref, wo_ref, g1_ref, b1_ref,
               wfi_ref, wfo_ref, g2_ref, b2_ref, o_ref):
    mgs = (mg0_ref, mg1_ref, mg2_ref)
    merged = None
    for n in range(N_BRANCH):
        y = ys_ref[:, n * BRANCH_W:(n + 1) * BRANCH_W].astype(BF16)
        z = jnp.dot(y, wb_ref[n], preferred_element_type=F32)
        t = jax.nn.sigmoid(mgs[n][...]) * z
        merged = t if merged is None else merged + t
    mix = jnp.dot(merged.astype(BF16), wo_ref[...], preferred_element_type=F32)
    h1 = _ln(ALPHA * h_ref[...] + mix, g1_ref[...], b1_ref[...])
    h1b = h1.astype(BF16)
    ff = None
    for j in range(FFN_HIDDEN // FFN_HC):
        gate = jnp.dot(h1b, wfi_ref[:, j * FFN_HC:(j + 1) * FFN_HC], preferred_element_type=F32)
        up = jnp.dot(h1b, wfi_ref[:, FFN_HIDDEN + j * FFN_HC:FFN_HIDDEN + (j + 1) * FFN_HC],
                     preferred_element_type=F32)
        a = (gate * jax.nn.sigmoid(gate) * up).astype(BF16)
        t = jnp.dot(a, wfo_ref[j * FFN_HC:(j + 1) * FFN_HC, :], preferred_element_type=F32)
        ff = t if ff is None else ff + t
    o_ref[...] = _ln(ALPHA * h1 + ff, g2_ref[...], b2_ref[...])


def _const_spec(shape):
    nd = len(shape)
    return pl.BlockSpec(shape, lambda i: (0,) * nd, pipeline_mode=pl.Buffered(1))


def _post(ys, proj, h, wb, wo, g1, b1, wfi, wfo, g2, b2):
    T = h.shape[0]
    tm = 256
    mg_blk = COL_MG // D_MODEL
    return pl.pallas_call(
        _post_body,
        grid=(T // tm,),
        in_specs=[pl.BlockSpec((tm, N_BRANCH * BRANCH_W), lambda i: (i, 0)),
                  pl.BlockSpec((tm, D_MODEL), lambda i: (i, mg_blk)),
                  pl.BlockSpec((tm, D_MODEL), lambda i: (i, mg_blk + 1)),
                  pl.BlockSpec((tm, D_MODEL), lambda i: (i, mg_blk + 2)),
                  pl.BlockSpec((tm, D_MODEL), lambda i: (i, 0)),
                  _const_spec((N_BRANCH, BRANCH_W, D_MODEL)),
                  _const_spec((D_MODEL, D_MODEL)),
                  _const_spec((1, D_MODEL)), _const_spec((1, D_MODEL)),
                  _const_spec((D_MODEL, 2 * FFN_HIDDEN)),
                  _const_spec((FFN_HIDDEN, D_MODEL)),
                  _const_spec((1, D_MODEL)), _const_spec((1, D_MODEL))],
        out_specs=pl.BlockSpec((tm, D_MODEL), lambda i: (i, 0)),
        out_shape=jax.ShapeDtypeStruct((T, D_MODEL), F32),
        compiler_params=pltpu.CompilerParams(
            dimension_semantics=("arbitrary",), vmem_limit_bytes=VMEM_LIMIT),
        name="merge_ffn",
    )(ys, proj, proj, proj, h, wb, wo, g1, b1, wfi, wfo, g2, b2)


def _heads(t, h):
    B, L, _ = t.shape
    return t.reshape(B, L, h, -1).transpose(0, 2, 1, 3)


def _head_rms_merge(o, g):
    o = o * lax.rsqrt(jnp.mean(o * o, -1, keepdims=True) + NORM_EPS)
    B, H, L, d = o.shape
    return o.transpose(0, 2, 1, 3).reshape(B, L, H * d) * g


def _l2norm(t):
    return t * lax.rsqrt(jnp.sum(t * t, -1, keepdims=True) + NORM_EPS)


def _masked_exp(mask, t):
    return jnp.where(mask, jnp.exp(jnp.where(mask, t, 0.0)), 0.0)


def _chunk(t, c):
    B, H, L = t.shape[:3]
    t = t.reshape((B, H, L // c, c) + t.shape[3:])
    return jnp.moveaxis(t, 2, 0)


def _unchunk(o):
    n, B, H, c = o.shape[:4]
    o = jnp.moveaxis(o, 0, 2)
    return o.reshape((B, H, n * c) + o.shape[4:])


def _hgrn2_scan(q, k, v, logf, S0):
    c = math.gcd(q.shape[2], CHUNK)
    incl = jnp.tril(jnp.ones((c, c), bool))

    def step(S, blk):
        qc, kc, vc, fc = blk
        b = jnp.cumsum(fc, axis=2)
        diff = b[:, :, :, None, :] - b[:, :, None, :, :]
        dec = _masked_exp(incl[:, :, None], diff)
        att = jnp.einsum('bhtk,bhsk,bhtsk->bhts', qc, kc, dec)
        o = jnp.einsum('bhtk,bhkv->bhtv', qc * jnp.exp(b), S) + jnp.einsum('bhts,bhsv->bhtv', att, vc)
        bl = b[:, :, -1, :]
        S_new = jnp.exp(bl)[..., None] * S + jnp.einsum('bhsk,bhsv->bhkv', kc * jnp.exp(bl[:, :, None, :] - b), vc)
        return S_new, o

    S, o = lax.scan(step, S0, (_chunk(q, c), _chunk(k, c), _chunk(v, c), _chunk(logf, c)))
    return _unchunk(o), S


def _mlstm_scan(q, k, v, ig, lf, C0, n0, m0):
    c = math.gcd(q.shape[2], CHUNK)
    incl = jnp.tril(jnp.ones((c, c), bool))

    def step(carry, blk):
        Cm, n, m = carry
        qc, kc, vc, ic, fc = blk
        b = jnp.cumsum(fc, axis=-1)
        logd = jnp.where(incl, b[..., :, None] - b[..., None, :] + ic[..., None, :], NEG_BIG)
        m_inter = b + m[..., None]
        m_t = jnp.maximum(m_inter, jnp.max(logd, -1))
        w_inter = jnp.exp(m_inter - m_t)
        sc = jnp.einsum('bhtk,bhsk->bhts', qc, kc) * _masked_exp(incl, logd - m_t[..., None])
        num = w_inter[..., None] * jnp.einsum('bhvk,bhtk->bhtv', Cm, qc) + jnp.einsum('bhts,bhsv->bhtv', sc, vc)
        den = w_inter * jnp.einsum('bhk,bhtk->bht', n, qc) + jnp.sum(sc, -1)
        h = num / jnp.maximum(jnp.abs(den), jnp.exp(-m_t))[..., None]
        g = b[..., -1:] - b + ic
        m_new = jnp.maximum(b[..., -1] + m, jnp.max(g, -1))
        w_old = jnp.exp(b[..., -1] + m - m_new)
        wk = jnp.exp(g - m_new[..., None])
        C_new = w_old[..., None, None] * Cm + jnp.einsum('bhs,bhsv,bhsk->bhvk', wk, vc, kc)
        n_new = w_old[..., None] * n + jnp.einsum('bhs,bhsk->bhk', wk, kc)
        return (C_new, n_new, m_new), h

    (Cm, n, m), o = lax.scan(step, (C0, n0, m0),
                             (_chunk(q, c), _chunk(k, c), _chunk(v, c), _chunk(ig, c), _chunk(lf, c)))
    return _unchunk(o), Cm, n, m


def _gdn_scan(q, k, v, beta, g, S0):
    c = math.gcd(q.shape[2], CHUNK)
    strict = jnp.tril(jnp.ones((c, c), bool), -1)
    incl = jnp.tril(jnp.ones((c, c), bool))
    eye = jnp.eye(c, dtype=jnp.float32)
    dv = v.shape[-1]

    def step(S, blk):
        qc, kc, vc, bc, gc = blk
        b = jnp.cumsum(gc, axis=-1)
        diff = b[..., :, None] - b[..., None, :]
        a = bc[..., :, None] * jnp.einsum('bhtk,bhsk->bhts', kc, kc) * _masked_exp(strict, diff)
        rhs = jnp.concatenate([vc * bc[..., None], kc * (bc * jnp.exp(b))[..., None]], axis=-1)
        sol = lax.linalg.triangular_solve(eye + a, rhs, left_side=True, lower=True, unit_diagonal=True)
        v_new = sol[..., :dv] - jnp.einsum('bhtk,bhkv->bhtv', sol[..., dv:], S)
        att = jnp.einsum('bhtk,bhsk->bhts', qc, kc) * _masked_exp(incl, diff)
        o = jnp.einsum('bhtk,bhkv->bhtv', qc * jnp.exp(b)[..., None], S) + jnp.einsum('bhts,bhsv->bhtv', att, v_new)
        bl = b[..., -1]
        S_new = jnp.exp(bl)[..., None, None] * S + jnp.einsum('bhsk,bhsv->bhkv', kc * jnp.exp(bl[..., None] - b)[..., None], v_new)
        return S_new, o

    S, o = lax.scan(step, S0, (_chunk(q, c), _chunk(k, c), _chunk(v, c), _chunk(beta, c), _chunk(g, c)))
    return _unchunk(o), S


def _causal_conv(xc, buf, w):
    L = xc.shape[1]
    xp = jnp.concatenate([buf, xc], axis=1)
    out = xp[:, 0:L] * w[0]
    for j in range(1, CONV_W):
        out = out + xp[:, j:j + L] * w[j]
    return out, xp[:, L:]


def _mixers(proj, B, L, st, lb, a_norm_g, b_mi, b_mf, b_norm_g, conv_w, a_log, dt_bias, c_norm_g):
    st_a, st_c, st_n, st_m, st_g, st_conv = st
    p = proj.reshape(B, L, N_PROJ)
    aq, af, ai, ag = (p[..., COL_A + i * 512:COL_A + (i + 1) * 512] for i in range(4))
    cx = p[..., COL_C:COL_C + CONV_CH]
    cg = p[..., COL_C + CONV_CH:COL_C + 2048]
    bq = p[..., COL_B:COL_B + 256]
    bk = p[..., COL_B + 256:COL_B + 512]
    bv = p[..., COL_B + 512:COL_B + 1024]
    bo = p[..., COL_B + 1024:COL_B + 1536]
    bi = p[..., COL_S:COL_S + 4]
    bf = p[..., COL_S + 4:COL_S + 8]
    cb = p[..., COL_S + 8:COL_S + 12]
    ca = p[..., COL_S + 12:COL_S + 16]

    a_logf = jnp.log(lb + (1.0 - lb) * jax.nn.sigmoid(af))
    a_k = (1.0 - lb) * jax.nn.sigmoid(-af)
    a_o, a_S = _hgrn2_scan(_heads(jax.nn.silu(aq), A_HEADS), _heads(a_k, A_HEADS),
                           _heads(ai, A_HEADS), _heads(a_logf, A_HEADS), st_a)
    y_a = _head_rms_merge(a_o, a_norm_g) * jax.nn.silu(ag)

    b_ig = (bi + b_mi).transpose(0, 2, 1)
    b_lf = jax.nn.log_sigmoid(bf + b_mf).transpose(0, 2, 1)
    b_o, b_C, b_n, b_m = _mlstm_scan(_heads(bq, B_HEADS) * B_QK ** -0.5, _heads(bk, B_HEADS),
                                     _heads(bv, B_HEADS), b_ig, b_lf, st_c, st_n, st_m)
    y_b = _head_rms_merge(b_o, b_norm_g) * jax.nn.sigmoid(bo)

    cconv, cbuf = _causal_conv(cx, st_conv, conv_w)
    cconv = jax.nn.silu(cconv)
    cq, ck, cv = jnp.split(cconv, [C_HEADS * C_KEY, 2 * C_HEADS * C_KEY], axis=-1)
    cq = _l2norm(_heads(cq, C_HEADS)) * C_KEY ** -0.5
    ck = _l2norm(_heads(ck, C_HEADS))
    c_beta = jax.nn.sigmoid(cb).transpose(0, 2, 1)
    c_g = (-jnp.exp(a_log) * jax.nn.softplus(ca + dt_bias)).transpose(0, 2, 1)
    c_o, c_S = _gdn_scan(cq, ck, _heads(cv, C_HEADS), c_beta, c_g, st_g)
    y_c = _head_rms_merge(c_o, c_norm_g) * jax.nn.silu(cg)

    ys = jnp.concatenate([y_a, y_b, y_c], axis=-1).reshape(B * L, N_BRANCH * BRANCH_W)
    return ys, (a_S, b_C, b_n, b_m, c_S, cbuf)


def kernel(x_prompt, x_sample, state_hgrn, state_mlstm_c, state_mlstm_n, state_mlstm_m, state_gdn,
           state_gdn_conv, w_in, lb_logits, a_norm_g, b_mi, b_mf, b_norm_g, conv_w, a_log, dt_bias,
           c_norm_g, w_branch, w_out, ln1_g, ln1_b, w_ffn_in, w_ffn_out, ln2_g, ln2_b):
    Bp, Lp, _ = x_prompt.shape
    Bs, Ls, _ = x_sample.shape
    Tp, Ts = Bp * Lp, Bs * Ls

    w_in_r = _reorder_w_in(w_in)
    wb = w_branch.astype(BF16)
    wo = w_out.astype(BF16)
    wfi = w_ffn_in.astype(BF16)
    wfo = w_ffn_out.astype(BF16)
    lb_w = jax.nn.softmax(lb_logits.astype(F32), axis=0)
    lb_all = jnp.cumsum(lb_w, axis=0) - lb_w[0]

    h = jnp.concatenate([x_prompt.reshape(Tp, D_MODEL), x_sample.reshape(Ts, D_MODEL)], axis=0)
    new_p, new_s = [], []
    for l in range(DEPTH):
        proj = _proj(h, w_in_r[l])
        st_p = (jnp.zeros((Bp, A_HEADS, A_KEY, A_VAL), F32), jnp.zeros((Bp, B_HEADS, B_VAL, B_QK), F32),
                jnp.zeros((Bp, B_HEADS, B_QK), F32), jnp.zeros((Bp, B_HEADS), F32),
                jnp.zeros((Bp, C_HEADS, C_KEY, C_VAL), F32), jnp.zeros((Bp, CONV_W - 1, CONV_CH), F32))
        st_s = (state_hgrn[l], state_mlstm_c[l], state_mlstm_n[l], state_mlstm_m[l], state_gdn[l],
                state_gdn_conv[l])
        wts = (lb_all[l], a_norm_g[l], b_mi[l], b_mf[l], b_norm_g[l], conv_w[l], a_log[l], dt_bias[l],
               c_norm_g[l])
        ys_p, ns_p = _mixers(proj[:Tp], Bp, Lp, st_p, *wts)
        ys_s, ns_s = _mixers(proj[Tp:], Bs, Ls, st_s, *wts)
        ys = jnp.concatenate([ys_p, ys_s], axis=0)
        h = _post(ys, proj, h, wb[l], wo[l], ln1_g[l][None], ln1_b[l][None], wfi[l], wfo[l],
                  ln2_g[l][None], ln2_b[l][None])
        new_p.append(ns_p)
        new_s.append(ns_s)

    y_p = h[:Tp].reshape(Bp, Lp, D_MODEL)
    y_s = h[Tp:].reshape(Bs, Ls, D_MODEL)
    outs = [y_p, y_s]
    for i in range(6):
        outs.append(jnp.stack([ns[i] for ns in new_p]))
        outs.append(jnp.stack([ns[i] for ns in new_s]))
    return tuple(outs)
```

```python
import functools
import math

import jax
import jax.numpy as jnp
from jax import lax
from jax.experimental import pallas as pl
from jax.experimental.pallas import tpu as pltpu

F32 = jnp.float32
BF16 = jnp.bfloat16
MXU_DT = BF16

D_MODEL = 1024
DEPTH = 4
A_HEADS, A_KEY, A_VAL = 4, 128, 128
B_HEADS, B_QK, B_VAL = 4, 64, 128
C_HEADS, C_KEY, C_VAL = 4, 128, 128
CONV_W = 4
CONV_CH = C_HEADS * (2 * C_KEY + C_VAL)
N_BRANCH = 3
BRANCH_W = 512
FFN_HIDDEN = 2816
CHUNK = 64
ALPHA = (2 * DEPTH) ** 0.25
LN_EPS = 1e-5
NORM_EPS = 1e-6
NEG_BIG = -1e30

COL_A = 0
COL_C = 2048
COL_MG = 4096
COL_B = 7168
COL_S = 8704
N_PROJ = 8832
PROJ_TN = N_PROJ // 3

VMEM_LIMIT = 56 * 1024 * 1024


def _reorder_w_in(w_in):
    a = w_in[..., 0:2048]
    b = w_in[..., 2048:3584]
    s1 = w_in[..., 3584:3592]
    c = w_in[..., 3592:5640]
    s2 = w_in[..., 5640:5648]
    mg = w_in[..., 5648:8720]
    pad = jnp.zeros(w_in.shape[:-1] + (N_PROJ - COL_S - 16,), w_in.dtype)
    return jnp.concatenate([a, c, mg, b, s1, s2, pad], axis=-1).astype(BF16)


def _proj_body(x_ref, w_ref, o_ref):
    o_ref[...] = jnp.dot(x_ref[...].astype(BF16), w_ref[...], preferred_element_type=F32)


def _proj(h, w):
    T = h.shape[0]
    tm = 1024
    return pl.pallas_call(
        _proj_body,
        grid=(T // tm, N_PROJ // PROJ_TN),
        in_specs=[pl.BlockSpec((tm, D_MODEL), lambda i, j: (i, 0)),
                  pl.BlockSpec((D_MODEL, PROJ_TN), lambda i, j: (0, j))],
        out_specs=pl.BlockSpec((tm, PROJ_TN), lambda i, j: (i, j)),
        out_shape=jax.ShapeDtypeStruct((T, N_PROJ), F32),
        compiler_params=pltpu.CompilerParams(
            dimension_semantics=("arbitrary", "arbitrary"), vmem_limit_bytes=VMEM_LIMIT),
        name="in_proj",
    )(h, w)


def _ln(x, g, b):
    mu = jnp.mean(x, -1, keepdims=True)
    xc = x - mu
    var = jnp.mean(xc * xc, -1, keepdims=True)
    return xc * lax.rsqrt(var + LN_EPS) * g + b


FFN_HC = 256


def _post_body(ya_ref, yb_ref, yc_ref, mg0_ref, mg1_ref, mg2_ref, h_ref, wb_ref, wo_ref, g1_ref, b1_ref,
               wfi_ref, wfo_ref, g2_ref, b2_ref, o_ref):
    mgs = (mg0_ref, mg1_ref, mg2_ref)
    ys = (ya_ref, yb_ref, yc_ref)
    merged = None
    for n in range(N_BRANCH):
        y = ys[n][...].astype(BF16)
        z = jnp.dot(y, wb_ref[n], preferred_element_type=F32)
        t = jax.nn.sigmoid(mgs[n][...]) * z
        merged = t if merged is None else merged + t
    mix = jnp.dot(merged.astype(BF16), wo_ref[...], preferred_element_type=F32)
    h1 = _ln(ALPHA * h_ref[...] + mix, g1_ref[...], b1_ref[...])
    h1b = h1.astype(BF16)
    ff = None
    for j in range(FFN_HIDDEN // FFN_HC):
        gate = jnp.dot(h1b, wfi_ref[:, j * FFN_HC:(j + 1) * FFN_HC], preferred_element_type=F32)
        up = jnp.dot(h1b, wfi_ref[:, FFN_HIDDEN + j * FFN_HC:FFN_HIDDEN + (j + 1) * FFN_HC],
                     preferred_element_type=F32)
        a = (gate * jax.nn.sigmoid(gate) * up).astype(BF16)
        t = jnp.dot(a, wfo_ref[j * FFN_HC:(j + 1) * FFN_HC, :], preferred_element_type=F32)
        ff = t if ff is None else ff + t
    o_ref[...] = _ln(ALPHA * h1 + ff, g2_ref[...], b2_ref[...])


def _const_spec(shape):
    nd = len(shape)
    return pl.BlockSpec(shape, lambda i: (0,) * nd, pipeline_mode=pl.Buffered(1))


def _post(ya, yb, yc, proj, h, wb, wo, g1, b1, wfi, wfo, g2, b2):
    T = h.shape[0]
    tm = 256
    mg_blk = COL_MG // D_MODEL
    return pl.pallas_call(
        _post_body,
        grid=(T // tm,),
        in_specs=[pl.BlockSpec((tm, BRANCH_W), lambda i: (i, 0)),
                  pl.BlockSpec((tm, BRANCH_W), lambda i: (i, 0)),
                  pl.BlockSpec((tm, BRANCH_W), lambda i: (i, 0)),
                  pl.BlockSpec((tm, D_MODEL), lambda i: (i, mg_blk)),
                  pl.BlockSpec((tm, D_MODEL), lambda i: (i, mg_blk + 1)),
                  pl.BlockSpec((tm, D_MODEL), lambda i: (i, mg_blk + 2)),
                  pl.BlockSpec((tm, D_MODEL), lambda i: (i, 0)),
                  _const_spec((N_BRANCH, BRANCH_W, D_MODEL)),
                  _const_spec((D_MODEL, D_MODEL)),
                  _const_spec((1, D_MODEL)), _const_spec((1, D_MODEL)),
                  _const_spec((D_MODEL, 2 * FFN_HIDDEN)),
                  _const_spec((FFN_HIDDEN, D_MODEL)),
                  _const_spec((1, D_MODEL)), _const_spec((1, D_MODEL))],
        out_specs=pl.BlockSpec((tm, D_MODEL), lambda i: (i, 0)),
        out_shape=jax.ShapeDtypeStruct((T, D_MODEL), F32),
        compiler_params=pltpu.CompilerParams(
            dimension_semantics=("arbitrary",), vmem_limit_bytes=VMEM_LIMIT),
        name="merge_ffn",
    )(ya, yb, yc, proj, proj, proj, h, wb, wo, g1, b1, wfi, wfo, g2, b2)


SUB = 16
_HI = lax.Precision.HIGHEST


def _mm(a, b):
    return jnp.dot(a.astype(MXU_DT), b.astype(MXU_DT), preferred_element_type=F32)


def _mm_nt(a, b):
    return lax.dot_general(a.astype(MXU_DT), b.astype(MXU_DT), (((1,), (1,)), ((), ())),
                           preferred_element_type=F32)


def _mm_tn(a, b):
    return lax.dot_general(a.astype(MXU_DT), b.astype(MXU_DT), (((0,), (0,)), ((), ())),
                           preferred_element_type=F32)


def _mm_hi(a, b):
    return jnp.dot(a, b, precision=_HI, preferred_element_type=F32)


def _silu(x):
    return x * jax.nn.sigmoid(x)


def _softplus(x):
    return jnp.maximum(x, 0.0) + jnp.log1p(jnp.exp(-jnp.abs(x)))


def _masked_exp(mask, t):
    return jnp.where(mask, jnp.exp(jnp.where(mask, t, 0.0)), 0.0)


def _rms(o):
    return o * lax.rsqrt(jnp.mean(o * o, -1, keepdims=True) + NORM_EPS)


def _col2row(col, eye):
    return jnp.sum(jnp.where(eye, col, 0.0), axis=0, keepdims=True)


def _iotas(c):
    ri = lax.broadcasted_iota(jnp.int32, (c, c), 0)
    ci = lax.broadcasted_iota(jnp.int32, (c, c), 1)
    return ri, ci


def _hgrn_chunk(r0, c, a_ref, lb_ref, g_ref, y_ref, st_scr):
    rows = pl.ds(r0, c)
    ri, ci = _iotas(c)
    tri = (ri >= ci).astype(F32)
    lb = lb_ref[...]
    af = a_ref[rows, 512:1024]
    logf = jnp.log(lb + (1.0 - lb) * jax.nn.sigmoid(af))
    kk = (1.0 - lb) * jax.nn.sigmoid(-af)
    b = _mm_hi(tri, logf)
    eb = jnp.exp(b)
    q = _silu(a_ref[rows, 0:512])
    v = a_ref[rows, 1024:1536]
    ag = a_ref[rows, 1536:2048]
    sb = min(SUB, c)
    trow = lax.broadcasted_iota(jnp.int32, (sb, 1), 0)
    for h in range(A_HEADS):
        sl = slice(h * 128, (h + 1) * 128)
        st = st_scr[h]
        qh, kh, vh, bh = q[:, sl], kk[:, sl], v[:, sl], b[:, sl]
        o_inter = _mm_nt(qh * eb[:, sl], st)
        blocks = []
        for i in range(c // sb):
            rs = slice(i * sb, (i + 1) * sb)
            qi, bi = qh[rs], bh[rs]
            oi = o_inter[rs]
            if i > 0:
                r = bh[i * sb - 1:i * sb]
                att = _mm_nt(qi * jnp.exp(bi - r), kh[:i * sb] * jnp.exp(r - bh[:i * sb]))
                oi = oi + _mm(att, vh[:i * sb])
            ki, vi = kh[rs], vh[rs]
            for s in range(sb):
                msk = trow >= s
                p = _masked_exp(msk, bi - bi[s:s + 1]) * qi * ki[s:s + 1]
                oi = oi + jnp.sum(p, axis=-1, keepdims=True) * vi[s:s + 1]
            blocks.append(oi)
        o = blocks[0] if len(blocks) == 1 else jnp.concatenate(blocks, axis=0)
        bl = bh[c - 1:c]
        st_scr[h] = st * jnp.exp(bl) + _mm_tn(vh, kh * jnp.exp(bl - bh))
        y_ref[rows, sl] = _rms(o) * g_ref[:, sl] * _silu(ag[:, sl])


def _hgrn_body(a_ref, lb_ref, g_ref, s0_ref, y_ref, so_ref, st_scr, *, c, nchunks):
    l = pl.program_id(1)

    @pl.when(l == 0)
    def _():
        for h in range(A_HEADS):
            st_scr[h] = s0_ref[0, h].T

    if nchunks == 1:
        _hgrn_chunk(0, c, a_ref, lb_ref, g_ref, y_ref, st_scr)
    else:
        def step(i, carry):
            _hgrn_chunk(pl.multiple_of(i * c, c), c, a_ref, lb_ref, g_ref, y_ref, st_scr)
            return carry
        lax.fori_loop(0, nchunks, step, 0)

    @pl.when(l == pl.num_programs(1) - 1)
    def _():
        for h in range(A_HEADS):
            so_ref[0, h] = st_scr[h].T


def _hgrn(proj, B, L, tl, c, lb, g, s0):
    nl = L // tl
    return pl.pallas_call(
        functools.partial(_hgrn_body, c=c, nchunks=tl // c),
        grid=(B, nl),
        in_specs=[pl.BlockSpec((tl, 2048), lambda b, l: (b * nl + l, COL_A // 2048)),
                  pl.BlockSpec((1, 512), lambda b, l: (0, 0)),
                  pl.BlockSpec((1, 512), lambda b, l: (0, 0)),
                  pl.BlockSpec((1, A_HEADS, A_KEY, A_VAL), lambda b, l: (b, 0, 0, 0))],
        out_specs=[pl.BlockSpec((tl, 512), lambda b, l: (b * nl + l, 0)),
                   pl.BlockSpec((1, A_HEADS, A_KEY, A_VAL), lambda b, l: (b, 0, 0, 0))],
        out_shape=[jax.ShapeDtypeStruct((B * L, 512), F32),
                   jax.ShapeDtypeStruct((B, A_HEADS, A_KEY, A_VAL), F32)],
        scratch_shapes=[pltpu.VMEM((A_HEADS, A_VAL, A_KEY), F32)],
        compiler_params=pltpu.CompilerParams(
            dimension_semantics=("arbitrary", "arbitrary"), vmem_limit_bytes=VMEM_LIMIT),
        name="hgrn2",
    )(proj, lb, g, s0)


def _mlstm_chunk(r0, c, qk_ref, v_ref, og_ref, sm_ref, bias_ref, g_ref, y_ref, c_scr, n_scr, m_scr):
    rows = pl.ds(r0, c)
    ri, ci = _iotas(c)
    incl = ri >= ci
    eye = ri == ci
    tri = incl.astype(F32)
    pre = sm_ref[rows, :] + bias_ref[...]
    lf = -_softplus(-pre)
    bcum = _mm_hi(tri, lf)
    qk = qk_ref[rows, :]
    vv = v_ref[rows, :]
    og = og_ref[rows, :]
    for h in range(B_HEADS):
        qh = qk[:, h * 64:(h + 1) * 64] * (B_QK ** -0.5)
        kh = qk[:, 256 + h * 64:256 + (h + 1) * 64]
        vh = vv[:, h * 128:(h + 1) * 128]
        ic = pre[:, h:h + 1]
        bc = bcum[:, 4 + h:5 + h]
        m = m_scr[h:h + 1, 0:1]
        cm = c_scr[h]
        n = n_scr[h]
        b_row = _col2row(bc, eye)
        i_row = _col2row(ic, eye)
        logd = jnp.where(incl, bc - b_row + i_row, NEG_BIG)
        m_inter = bc + m
        m_t = jnp.maximum(m_inter, jnp.max(logd, -1, keepdims=True))
        w_inter = jnp.exp(m_inter - m_t)
        sc = _mm_nt(qh, kh) * _masked_exp(incl, logd - m_t)
        num = w_inter * _mm_nt(qh, cm) + _mm(sc, vh)
        den = w_inter * jnp.sum(qh * n, -1, keepdims=True) + jnp.sum(sc, -1, keepdims=True)
        hh = num / jnp.maximum(jnp.abs(den), jnp.exp(-m_t))
        bl = bc[c - 1:c]
        gg = bl - bc + ic
        m_new = jnp.maximum(bl + m, jnp.max(gg, 0, keepdims=True))
        w_old = jnp.exp(bl + m - m_new)
        wk = jnp.exp(gg - m_new)
        c_scr[h] = w_old * cm + _mm_tn(vh * wk, kh)
        n_scr[h] = w_old * n + jnp.sum(wk * kh, 0, keepdims=True)
        m_scr[h:h + 1, :] = jnp.broadcast_to(m_new, (1, 128))
        sl = slice(h * 128, (h + 1) * 128)
        y_ref[rows, sl] = _rms(hh) * g_ref[:, sl] * jax.nn.sigmoid(og[:, sl])


def _mlstm_body(qk_ref, v_ref, og_ref, sm_ref, bias_ref, g_ref, c0_ref, n0_ref, m0_ref,
                y_ref, co_ref, no_ref, mo_ref, c_scr, n_scr, m_scr, *, c, nchunks):
    l = pl.program_id(1)

    @pl.when(l == 0)
    def _():
        for h in range(B_HEADS):
            c_scr[h] = c0_ref[0, h]
            n_scr[h] = n0_ref[0, h:h + 1, :]
            m_scr[h:h + 1, :] = jnp.broadcast_to(m0_ref[0, h:h + 1, :], (1, 128))

    args = (qk_ref, v_ref, og_ref, sm_ref, bias_ref, g_ref, y_ref, c_scr, n_scr, m_scr)
    if nchunks == 1:
        _mlstm_chunk(0, c, *args)
    else:
        def step(i, carry):
            _mlstm_chunk(pl.multiple_of(i * c, c), c, *args)
            return carry
        lax.fori_loop(0, nchunks, step, 0)

    @pl.when(l == pl.num_programs(1) - 1)
    def _():
        for h in range(B_HEADS):
            co_ref[0, h] = c_scr[h]
            no_ref[0, h:h + 1, :] = n_scr[h]
            mo_ref[0, h:h + 1, :] = m_scr[h:h + 1, 0:1]


def _mlstm(proj, B, L, tl, c, bias_row, g, c0, n0, m0):
    nl = L // tl
    cb = COL_B // 512
    row = lambda b, l: b * nl + l
    return pl.pallas_call(
        functools.partial(_mlstm_body, c=c, nchunks=tl // c),
        grid=(B, nl),
        in_specs=[pl.BlockSpec((tl, 512), lambda b, l: (row(b, l), cb)),
                  pl.BlockSpec((tl, 512), lambda b, l: (row(b, l), cb + 1)),
                  pl.BlockSpec((tl, 512), lambda b, l: (row(b, l), cb + 2)),
                  pl.BlockSpec((tl, 128), lambda b, l: (row(b, l), COL_S // 128)),
                  pl.BlockSpec((1, 128), lambda b, l: (0, 0)),
                  pl.BlockSpec((1, 512), lambda b, l: (0, 0)),
                  pl.BlockSpec((1, B_HEADS, B_VAL, B_QK), lambda b, l: (b, 0, 0, 0)),
                  pl.BlockSpec((1, B_HEADS, B_QK), lambda b, l: (b, 0, 0)),
                  pl.BlockSpec((1, B_HEADS, 1), lambda b, l: (b, 0, 0))],
        out_specs=[pl.BlockSpec((tl, 512), lambda b, l: (row(b, l), 0)),
                   pl.BlockSpec((1, B_HEADS, B_VAL, B_QK), lambda b, l: (b, 0, 0, 0)),
                   pl.BlockSpec((1, B_HEADS, B_QK), lambda b, l: (b, 0, 0)),
                   pl.BlockSpec((1, B_HEADS, 1), lambda b, l: (b, 0, 0))],
        out_shape=[jax.ShapeDtypeStruct((B * L, 512), F32),
                   jax.ShapeDtypeStruct((B, B_HEADS, B_VAL, B_QK), F32),
                   jax.ShapeDtypeStruct((B, B_HEADS, B_QK), F32),
                   jax.ShapeDtypeStruct((B, B_HEADS, 1), F32)],
        scratch_shapes=[pltpu.VMEM((B_HEADS, B_VAL, B_QK), F32),
                        pltpu.VMEM((B_HEADS, 1, B_QK), F32),
                        pltpu.VMEM((8, 128), F32)],
        compiler_params=pltpu.CompilerParams(
            dimension_semantics=("arbitrary", "arbitrary"), vmem_limit_bytes=VMEM_LIMIT),
        name="mlstm",
    )(proj, proj, proj, proj, bias_row, g, c0, n0, m0)


def _unit_lower_inverse(a, c, ri, ci):
    eye = (ri == ci).astype(F32)
    if c <= SUB:
        ad, aoff = a, None
    else:
        same = (ri // SUB) == (ci // SUB)
        ad = jnp.where(same, a, 0.0)
        aoff = a - ad
    t = eye - ad
    p = ad
    n = 2
    while n < min(c, SUB):
        p = _mm_hi(p, p)
        t = _mm_hi(t, eye + p)
        n *= 2
    if aoff is None:
        return t
    assert c // SUB == 4
    nn = _mm_hi(t, aoff)
    n2 = _mm_hi(nn, nn)
    return _mm_hi(eye - nn + n2 - _mm_hi(nn, n2), t)


def _gdn_chunk(r0, c, cv_scr, x_ref, sm_ref, arow_ref, dtrow_ref, g_ref, y_ref, s_scr):
    rows = pl.ds(r0, c)
    ri, ci = _iotas(c)
    incl = ri >= ci
    strict = ri > ci
    eye = ri == ci
    tri = incl.astype(F32)
    sm = sm_ref[rows, :]
    beta_all = jax.nn.sigmoid(sm)
    g_all = arow_ref[...] * _softplus(sm + dtrow_ref[...])
    bcum = _mm_hi(tri, g_all)
    cg = x_ref[rows, CONV_CH:2048]
    for h in range(C_HEADS):
        sl = slice(h * 128, (h + 1) * 128)
        qh = cv_scr[rows, h * 128:(h + 1) * 128]
        kh = cv_scr[rows, 512 + h * 128:512 + (h + 1) * 128]
        vh = cv_scr[rows, 1024 + h * 128:1024 + (h + 1) * 128]
        qh = qh * lax.rsqrt(jnp.sum(qh * qh, -1, keepdims=True) + NORM_EPS) * (C_KEY ** -0.5)
        kh = kh * lax.rsqrt(jnp.sum(kh * kh, -1, keepdims=True) + NORM_EPS)
        beta = beta_all[:, 8 + h:9 + h]
        bc = bcum[:, 12 + h:13 + h]
        s = s_scr[h]
        diff = bc - _col2row(bc, eye)
        a = beta * _mm_nt(kh, kh) * _masked_exp(strict, diff)
        ebc = jnp.exp(bc)
        rhs = jnp.concatenate([vh * beta, kh * (beta * ebc)], axis=-1)
        sol = _mm_hi(_unit_lower_inverse(a, c, ri, ci), rhs)
        v_new = sol[:, :C_VAL] - _mm(sol[:, C_VAL:], s)
        att = _mm_nt(qh, kh) * _masked_exp(incl, diff)
        o = _mm(qh * ebc, s) + _mm(att, v_new)
        bl = bc[c - 1:c]
        s_scr[h] = jnp.exp(bl) * s + _mm_tn(kh * jnp.exp(bl - bc), v_new)
        y_ref[rows, sl] = _rms(o) * g_ref[:, sl] * _silu(cg[:, sl])


def _gdn_body(x_ref, sm_ref, cw_ref, arow_ref, dtrow_ref, g_ref, s0_ref, cs_ref,
              y_ref, so_ref, cso_ref, s_scr, xb_scr, cv_scr, *, tl, c, nchunks):
    l = pl.program_id(1)

    @pl.when(l == 0)
    def _():
        for h in range(C_HEADS):
            s_scr[h] = s0_ref[0, h]
        xb_scr[0:8, :] = cs_ref[0]

    xb_scr[8:8 + tl, :] = x_ref[:, 0:CONV_CH]
    conv = xb_scr[5:5 + tl, :] * cw_ref[0:1, :]
    for j in range(1, CONV_W):
        conv = conv + xb_scr[5 + j:5 + j + tl, :] * cw_ref[j:j + 1, :]
    cv_scr[...] = _silu(conv)

    args = (cv_scr, x_ref, sm_ref, arow_ref, dtrow_ref, g_ref, y_ref, s_scr)
    if nchunks == 1:
        _gdn_chunk(0, c, *args)
    else:
        def step(i, carry):
            _gdn_chunk(pl.multiple_of(i * c, c), c, *args)
            return carry
        lax.fori_loop(0, nchunks, step, 0)

    @pl.when(l == pl.num_programs(1) - 1)
    def _():
        for h in range(C_HEADS):
            so_ref[0, h] = s_scr[h]
        cso_ref[0] = xb_scr[tl + 5:tl + 8, :]

    xb_scr[0:8, :] = xb_scr[tl:tl + 8, :]


def _gdn(proj, B, L, tl, c, conv_w, arow, dtrow, g, s0, cs_pad):
    nl = L // tl
    row = lambda b, l: b * nl + l
    return pl.pallas_call(
        functools.partial(_gdn_body, tl=tl, c=c, nchunks=tl // c),
        grid=(B, nl),
        in_specs=[pl.BlockSpec((tl, 2048), lambda b, l: (row(b, l), COL_C // 2048)),
                  pl.BlockSpec((tl, 128), lambda b, l: (row(b, l), COL_S // 128)),
                  pl.BlockSpec((CONV_W, CONV_CH), lambda b, l: (0, 0)),
                  pl.BlockSpec((1, 128), lambda b, l: (0, 0)),
                  pl.BlockSpec((1, 128), lambda b, l: (0, 0)),
                  pl.BlockSpec((1, 512), lambda b, l: (0, 0)),
                  pl.BlockSpec((1, C_HEADS, C_KEY, C_VAL), lambda b, l: (b, 0, 0, 0)),
                  pl.BlockSpec((1, 8, CONV_CH), lambda b, l: (b, 0, 0))],
        out_specs=[pl.BlockSpec((tl, 512), lambda b, l: (row(b, l), 0)),
                   pl.BlockSpec((1, C_HEADS, C_KEY, C_VAL), lambda b, l: (b, 0, 0, 0)),
                   pl.BlockSpec((1, CONV_W - 1, CONV_CH), lambda b, l: (b, 0, 0))],
        out_shape=[jax.ShapeDtypeStruct((B * L, 512), F32),
                   jax.ShapeDtypeStruct((B, C_HEADS, C_KEY, C_VAL), F32),
                   jax.ShapeDtypeStruct((B, CONV_W - 1, CONV_CH), F32)],
        scratch_shapes=[pltpu.VMEM((C_HEADS, C_KEY, C_VAL), F32),
                        pltpu.VMEM((tl + 8, CONV_CH), F32),
                        pltpu.VMEM((tl, CONV_CH), F32)],
        compiler_params=pltpu.CompilerParams(
            dimension_semantics=("arbitrary", "arbitrary"), vmem_limit_bytes=VMEM_LIMIT),
        name="gdn",
    )(proj, proj, conv_w, arow, dtrow, g, s0, cs_pad)


def _pad_row(vals, start):
    return jnp.zeros((1, 128), F32).at[0, start:start + vals.shape[0]].set(vals.astype(F32))


def _layer(h, B, L, tl, c, st, w, lw):
    st_a, st_c, st_n, st_m, st_g, st_conv = st
    proj = _proj(h, w["w_in"])
    ya, na = _hgrn(proj, B, L, tl, c, lw["lb"], lw["a_g"], st_a)
    yb, nc, nn, nm = _mlstm(proj, B, L, tl, c, lw["b_bias"], lw["b_g"], st_c, st_n, st_m[..., None])
    cs_pad = jnp.pad(st_conv, ((0, 0), (8 - (CONV_W - 1), 0), (0, 0)))
    yc, ng, nv = _gdn(proj, B, L, tl, c, lw["conv_w"], lw["c_arow"], lw["c_dtrow"], lw["c_g"], st_g, cs_pad)
    h = _post(ya, yb, yc, proj, h, w["wb"], w["wo"], lw["ln1_g"], lw["ln1_b"], w["wfi"], w["wfo"],
              lw["ln2_g"], lw["ln2_b"])
    return h, (na, nc, nn, nm[..., 0], ng, nv)


def kernel(x_prompt, x_sample, state_hgrn, state_mlstm_c, state_mlstm_n, state_mlstm_m, state_gdn,
           state_gdn_conv, w_in, lb_logits, a_norm_g, b_mi, b_mf, b_norm_g, conv_w, a_log, dt_bias,
           c_norm_g, w_branch, w_out, ln1_g, ln1_b, w_ffn_in, w_ffn_out, ln2_g, ln2_b):
    Bp, Lp, _ = x_prompt.shape
    Bs, Ls, _ = x_sample.shape
    cp, cs = math.gcd(Lp, CHUNK), math.gcd(Ls, CHUNK)
    tlp = min(Lp, 256)

    w_in_r = _reorder_w_in(w_in)
    wb = w_branch.astype(BF16)
    wo = w_out.astype(BF16)
    wfi = w_ffn_in.astype(BF16)
    wfo = w_ffn_out.astype(BF16)
    lb_w = jax.nn.softmax(lb_logits.astype(F32), axis=0)
    lb_all = jnp.cumsum(lb_w, axis=0) - lb_w[0]

    h_p = x_prompt.reshape(Bp * Lp, D_MODEL)
    h_s = x_sample.reshape(Bs * Ls, D_MODEL)
    zero_st = (jnp.zeros((Bp, A_HEADS, A_KEY, A_VAL), F32), jnp.zeros((Bp, B_HEADS, B_VAL, B_QK), F32),
               jnp.zeros((Bp, B_HEADS, B_QK), F32), jnp.zeros((Bp, B_HEADS), F32),
               jnp.zeros((Bp, C_HEADS, C_KEY, C_VAL), F32), jnp.zeros((Bp, CONV_W - 1, CONV_CH), F32))
    new_p, new_s = [], []
    for l in range(DEPTH):
        w = dict(w_in=w_in_r[l], wb=wb[l], wo=wo[l], wfi=wfi[l], wfo=wfo[l])
        lw = dict(lb=lb_all[l][None], a_g=a_norm_g[l][None].astype(F32),
                  b_bias=_pad_row(jnp.concatenate([b_mi[l], b_mf[l]]), 0), b_g=b_norm_g[l][None].astype(F32),
                  conv_w=conv_w[l].astype(F32), c_arow=_pad_row(-jnp.exp(a_log[l].astype(F32)), 12),
                  c_dtrow=_pad_row(dt_bias[l], 12), c_g=c_norm_g[l][None].astype(F32),
                  ln1_g=ln1_g[l][None].astype(F32), ln1_b=ln1_b[l][None].astype(F32),
                  ln2_g=ln2_g[l][None].astype(F32), ln2_b=ln2_b[l][None].astype(F32))
        st_s = (state_hgrn[l], state_mlstm_c[l], state_mlstm_n[l], state_mlstm_m[l], state_gdn[l],
                state_gdn_conv[l])
        h_p, ns_p = _layer(h_p, Bp, Lp, tlp, cp, zero_st, w, lw)
        h_s, ns_s = _layer(h_s, Bs, Ls, Ls, cs, st_s, w, lw)
        new_p.append(ns_p)
        new_s.append(ns_s)

    outs = [h_p.reshape(Bp, Lp, D_MODEL), h_s.reshape(Bs, Ls, D_MODEL)]
    for i in range(6):
        outs.append(jnp.stack([ns[i] for ns in new_p]))
        outs.append(jnp.stack([ns[i] for ns in new_s]))
    return tuple(outs)
```

```python
import functools
import math

import jax
import jax.numpy as jnp
from jax import lax
from jax.experimental import pallas as pl
from jax.experimental.pallas import tpu as pltpu

F32 = jnp.float32
BF16 = jnp.bfloat16
MXU_DT = BF16

D_MODEL = 1024
DEPTH = 4
A_HEADS, A_KEY, A_VAL = 4, 128, 128
B_HEADS, B_QK, B_VAL = 4, 64, 128
C_HEADS, C_KEY, C_VAL = 4, 128, 128
CONV_W = 4
CONV_CH = C_HEADS * (2 * C_KEY + C_VAL)
N_BRANCH = 3
BRANCH_W = 512
FFN_HIDDEN = 2816
CHUNK = 64
ALPHA = (2 * DEPTH) ** 0.25
LN_EPS = 1e-5
NORM_EPS = 1e-6
NEG_BIG = -1e30

COL_A = 0
COL_C = 2048
COL_MG = 4096
COL_B = 7168
COL_S = 8704
N_PROJ = 8832
PROJ_TN = N_PROJ // 3

VMEM_LIMIT = 56 * 1024 * 1024


def _reorder_w_in(w_in):
    a = w_in[..., 0:2048]
    b = w_in[..., 2048:3584]
    s1 = w_in[..., 3584:3592]
    c = w_in[..., 3592:5640]
    s2 = w_in[..., 5640:5648]
    mg = w_in[..., 5648:8720]
    pad = jnp.zeros(w_in.shape[:-1] + (N_PROJ - COL_S - 16,), w_in.dtype)
    return jnp.concatenate([a, c, mg, b, s1, s2, pad], axis=-1).astype(BF16)


def _proj_body(x_ref, w_ref, o_ref):
    o_ref[...] = jnp.dot(x_ref[...].astype(BF16), w_ref[...], preferred_element_type=F32)


def _proj(h, w, l):
    T = h.shape[0]
    tm = min(1024, T)
    return pl.pallas_call(
        _proj_body,
        grid=(T // tm, N_PROJ // PROJ_TN),
        in_specs=[pl.BlockSpec((tm, D_MODEL), lambda i, j: (i, 0)),
                  pl.BlockSpec((None, D_MODEL, PROJ_TN), lambda i, j: (l, 0, j))],
        out_specs=pl.BlockSpec((tm, PROJ_TN), lambda i, j: (i, j)),
        out_shape=jax.ShapeDtypeStruct((T, N_PROJ), F32),
        compiler_params=pltpu.CompilerParams(
            dimension_semantics=("arbitrary", "arbitrary"), vmem_limit_bytes=VMEM_LIMIT),
        name="in_proj",
    )(h, w)


def _ln(x, g, b):
    mu = jnp.mean(x, -1, keepdims=True)
    xc = x - mu
    var = jnp.mean(xc * xc, -1, keepdims=True)
    return xc * lax.rsqrt(var + LN_EPS) * g + b


FFN_HC = 256


def _post_body(ya_ref, yb_ref, yc_ref, mg0_ref, mg1_ref, mg2_ref, h_ref, wb_ref, wo_ref, g1_ref, b1_ref,
               wfi_ref, wfo_ref, g2_ref, b2_ref, o_ref):
    mgs = (mg0_ref, mg1_ref, mg2_ref)
    ys = (ya_ref, yb_ref, yc_ref)
    merged = None
    for n in range(N_BRANCH):
        y = ys[n][...].astype(BF16)
        z = jnp.dot(y, wb_ref[n], preferred_element_type=F32)
        t = jax.nn.sigmoid(mgs[n][...]) * z
        merged = t if merged is None else merged + t
    mix = jnp.dot(merged.astype(BF16), wo_ref[...], preferred_element_type=F32)
    h1 = _ln(ALPHA * h_ref[...] + mix, g1_ref[...], b1_ref[...])
    h1b = h1.astype(BF16)
    ff = None
    for j in range(FFN_HIDDEN // FFN_HC):
        gate = jnp.dot(h1b, wfi_ref[:, j * FFN_HC:(j + 1) * FFN_HC], preferred_element_type=F32)
        up = jnp.dot(h1b, wfi_ref[:, FFN_HIDDEN + j * FFN_HC:FFN_HIDDEN + (j + 1) * FFN_HC],
                     preferred_element_type=F32)
        a = (gate * jax.nn.sigmoid(gate) * up).astype(BF16)
        t = jnp.dot(a, wfo_ref[j * FFN_HC:(j + 1) * FFN_HC, :], preferred_element_type=F32)
        ff = t if ff is None else ff + t
    o_ref[...] = _ln(ALPHA * h1 + ff, g2_ref[...], b2_ref[...])


def _layer_spec(shape, l):
    nd = len(shape)
    return pl.BlockSpec((None,) + shape, lambda i: (l,) + (0,) * nd, pipeline_mode=pl.Buffered(1))


def _post(ya, yb, yc, proj, h, wb, wo, g1, b1, wfi, wfo, g2, b2, l):
    T = h.shape[0]
    tm = min(256, T)
    mg_blk = COL_MG // D_MODEL
    return pl.pallas_call(
        _post_body,
        grid=(T // tm,),
        in_specs=[pl.BlockSpec((tm, BRANCH_W), lambda i: (i, 0)),
                  pl.BlockSpec((tm, BRANCH_W), lambda i: (i, 0)),
                  pl.BlockSpec((tm, BRANCH_W), lambda i: (i, 0)),
                  pl.BlockSpec((tm, D_MODEL), lambda i: (i, mg_blk)),
                  pl.BlockSpec((tm, D_MODEL), lambda i: (i, mg_blk + 1)),
                  pl.BlockSpec((tm, D_MODEL), lambda i: (i, mg_blk + 2)),
                  pl.BlockSpec((tm, D_MODEL), lambda i: (i, 0)),
                  _layer_spec((N_BRANCH, BRANCH_W, D_MODEL), l),
                  _layer_spec((D_MODEL, D_MODEL), l),
                  _layer_spec((1, D_MODEL), l), _layer_spec((1, D_MODEL), l),
                  _layer_spec((D_MODEL, 2 * FFN_HIDDEN), l),
                  _layer_spec((FFN_HIDDEN, D_MODEL), l),
                  _layer_spec((1, D_MODEL), l), _layer_spec((1, D_MODEL), l)],
        out_specs=pl.BlockSpec((tm, D_MODEL), lambda i: (i, 0)),
        out_shape=jax.ShapeDtypeStruct((T, D_MODEL), F32),
        compiler_params=pltpu.CompilerParams(
            dimension_semantics=("arbitrary",), vmem_limit_bytes=VMEM_LIMIT),
        name="merge_ffn",
    )(ya, yb, yc, proj, proj, proj, h, wb, wo, g1, b1, wfi, wfo, g2, b2)


SAMPLE_G = 8
SUB = 16
SUB_A = 16
_HI = lax.Precision.HIGHEST


def _mm(a, b):
    return jnp.dot(a.astype(MXU_DT), b.astype(MXU_DT), preferred_element_type=F32)


def _mm_nt(a, b):
    return lax.dot_general(a.astype(MXU_DT), b.astype(MXU_DT), (((1,), (1,)), ((), ())),
                           preferred_element_type=F32)


def _mm_tn(a, b):
    return lax.dot_general(a.astype(MXU_DT), b.astype(MXU_DT), (((0,), (0,)), ((), ())),
                           preferred_element_type=F32)


def _mm_hi(a, b):
    return jnp.dot(a, b, precision=_HI, preferred_element_type=F32)


def _silu(x):
    return x * jax.nn.sigmoid(x)


def _softplus(x):
    return jnp.maximum(x, 0.0) + jnp.log1p(jnp.exp(-jnp.abs(x)))


def _masked_exp(mask, t):
    return jnp.where(mask, jnp.exp(jnp.where(mask, t, 0.0)), 0.0)


def _rms(o):
    return o * lax.rsqrt(jnp.mean(o * o, -1, keepdims=True) + NORM_EPS)


def _col2row(col, eye):
    return jnp.sum(jnp.where(eye, col, 0.0), axis=0, keepdims=True)


def _iotas(c):
    ri = lax.broadcasted_iota(jnp.int32, (c, c), 0)
    ci = lax.broadcasted_iota(jnp.int32, (c, c), 1)
    return ri, ci


def _hgrn_chunk(r0, c, a_ref, lb_ref, g_ref, y_ref, st_scr):
    rows = pl.ds(r0, c)
    ri, ci = _iotas(c)
    tri = (ri >= ci).astype(F32)
    lb = lb_ref[...]
    af = a_ref[rows, 512:1024]
    logf = jnp.log(lb + (1.0 - lb) * jax.nn.sigmoid(af))
    kk = (1.0 - lb) * jax.nn.sigmoid(-af)
    b = _mm_hi(tri, logf)
    eb = jnp.exp(b)
    q = _silu(a_ref[rows, 0:512])
    v = a_ref[rows, 1024:1536]
    ag = a_ref[rows, 1536:2048]
    sb = min(SUB_A, c)
    trow = lax.broadcasted_iota(jnp.int32, (sb, 1), 0)
    for h in range(A_HEADS):
        sl = slice(h * 128, (h + 1) * 128)
        st = st_scr[h]
        qh, kh, vh, bh = q[:, sl], kk[:, sl], v[:, sl], b[:, sl]
        o_inter = _mm_nt(qh * eb[:, sl], st)
        blocks = []
        for i in range(c // sb):
            rs = slice(i * sb, (i + 1) * sb)
            qi, bi = qh[rs], bh[rs]
            oi = o_inter[rs]
            if i > 0:
                r = bh[i * sb - 1:i * sb]
                att = _mm_nt(qi * jnp.exp(bi - r), kh[:i * sb] * jnp.exp(r - bh[:i * sb]))
                oi = oi + _mm(att, vh[:i * sb])
            ki, vi = kh[rs], vh[rs]
            for s in range(sb):
                msk = trow >= s
                p = _masked_exp(msk, bi - bi[s:s + 1]) * qi * ki[s:s + 1]
                oi = oi + jnp.sum(p, axis=-1, keepdims=True) * vi[s:s + 1]
            blocks.append(oi)
        o = blocks[0] if len(blocks) == 1 else jnp.concatenate(blocks, axis=0)
        bl = bh[c - 1:c]
        st_scr[h] = st * jnp.exp(bl) + _mm_tn(vh, kh * jnp.exp(bl - bh))
        y_ref[rows, sl] = _rms(o) * g_ref[:, sl] * _silu(ag[:, sl])


def _hgrn_body(a_ref, lb_ref, g_ref, s0_ref, y_ref, so_ref, st_scr, *, c, nchunks):
    l = pl.program_id(1)

    @pl.when(l == 0)
    def _():
        for h in range(A_HEADS):
            st_scr[h] = s0_ref[0, h].T

    if nchunks == 1:
        _hgrn_chunk(0, c, a_ref, lb_ref, g_ref, y_ref, st_scr)
    else:
        def step(i, carry):
            _hgrn_chunk(pl.multiple_of(i * c, c), c, a_ref, lb_ref, g_ref, y_ref, st_scr)
            return carry
        lax.fori_loop(0, nchunks, step, 0)

    @pl.when(l == pl.num_programs(1) - 1)
    def _():
        for h in range(A_HEADS):
            so_ref[0, h] = st_scr[h].T


def _hgrn(proj, B, L, tl, c, lb, g, s0, ls):
    nl = L // tl
    return pl.pallas_call(
        functools.partial(_hgrn_body, c=c, nchunks=tl // c),
        grid=(B, nl),
        in_specs=[pl.BlockSpec((tl, 2048), lambda b, l: (b * nl + l, COL_A // 2048)),
                  pl.BlockSpec((1, 512), lambda b, l: (0, 0)),
                  pl.BlockSpec((1, 512), lambda b, l: (0, 0)),
                  pl.BlockSpec((None, 1, A_HEADS, A_KEY, A_VAL), lambda b, l: (ls, b, 0, 0, 0))],
        out_specs=[pl.BlockSpec((tl, 512), lambda b, l: (b * nl + l, 0)),
                   pl.BlockSpec((1, A_HEADS, A_KEY, A_VAL), lambda b, l: (b, 0, 0, 0))],
        out_shape=[jax.ShapeDtypeStruct((B * L, 512), F32),
                   jax.ShapeDtypeStruct((B, A_HEADS, A_KEY, A_VAL), F32)],
        scratch_shapes=[pltpu.VMEM((A_HEADS, A_VAL, A_KEY), F32)],
        compiler_params=pltpu.CompilerParams(
            dimension_semantics=("arbitrary", "arbitrary"), vmem_limit_bytes=VMEM_LIMIT),
        name="hgrn2",
    )(proj, lb, g, s0)


def _mlstm_chunk(r0, c, G, qk_ref, v_ref, og_ref, sm_ref, bias_ref, g_ref, y_ref, c_scr, n_scr, m_scr):
    n = G * c
    R = B_HEADS * n
    rows = pl.ds(r0, n)
    rn, cn = _iotas(n)
    seq = (rn // c) == (cn // c)
    pre = sm_ref[rows, :] + bias_ref[...]
    lf = -_softplus(-pre)
    bcum = _mm_hi((seq & (rn >= cn)).astype(F32), lf)
    brev = _mm_hi((seq & (rn < cn)).astype(F32), lf)
    stack = lambda f: jnp.concatenate([f(h) for h in range(B_HEADS)], axis=0)
    q = stack(lambda h: qk_ref[rows, h * 64:(h + 1) * 64]) * (B_QK ** -0.5)
    k = stack(lambda h: qk_ref[rows, 256 + h * 64:256 + (h + 1) * 64])
    v = stack(lambda h: v_ref[rows, h * 128:(h + 1) * 128])
    ic = stack(lambda h: pre[:, h:h + 1])
    bc = stack(lambda h: bcum[:, 4 + h:5 + h])
    gg = stack(lambda h: brev[:, 4 + h:5 + h]) + ic
    m_st = m_scr[:, 0:1]

    qc, qn, wk = [], [], []
    for j in range(B_HEADS * G):
        sl = slice(j * c, (j + 1) * c)
        hh, g_ = divmod(j, G)
        cm = c_scr[j]
        nj = n_scr[j]
        m = m_scr[j * c:j * c + 1, 0:1]
        qc.append(_mm_nt(q[sl], cm))
        qn.append(jnp.sum(q[sl] * nj, -1, keepdims=True))
        bl = bcum[(g_ + 1) * c - 1:(g_ + 1) * c, 4 + hh:5 + hh]
        m_new = jnp.maximum(bl + m, jnp.max(gg[sl], 0, keepdims=True))
        w_old = jnp.exp(bl + m - m_new)
        wkj = jnp.exp(gg[sl] - m_new)
        c_scr[j] = w_old * cm + _mm_tn(v[sl] * wkj, k[sl])
        n_scr[j] = w_old * nj + jnp.sum(wkj * k[sl], 0, keepdims=True)
        m_scr[sl, :] = jnp.broadcast_to(m_new, (c, 128))
    qc = jnp.concatenate(qc, axis=0)
    qn = jnp.concatenate(qn, axis=0)

    ri, ci = _iotas(R)
    incl = ((ri // c) == (ci // c)) & (ri >= ci)
    eye = ri == ci
    logd = jnp.where(incl, bc - _col2row(bc, eye) + _col2row(ic, eye), NEG_BIG)
    m_inter = bc + m_st
    m_t = jnp.maximum(m_inter, jnp.max(logd, -1, keepdims=True))
    w_inter = jnp.exp(m_inter - m_t)
    sc = _mm_nt(q, k) * _masked_exp(incl, logd - m_t)
    num = w_inter * qc + _mm(sc, v)
    den = w_inter * qn + jnp.sum(sc, -1, keepdims=True)
    hh = num / jnp.maximum(jnp.abs(den), jnp.exp(-m_t))
    for h in range(B_HEADS):
        sl = slice(h * 128, (h + 1) * 128)
        y_ref[rows, sl] = _rms(hh[h * n:(h + 1) * n]) * g_ref[:, sl] * jax.nn.sigmoid(og_ref[rows, sl])


def _mlstm_body(qk_ref, v_ref, og_ref, sm_ref, bias_ref, g_ref, c0_ref, n0_ref, m0_ref,
                y_ref, co_ref, no_ref, mo_ref, c_scr, n_scr, m_scr, *, tl, c, G):
    l = pl.program_id(1)

    @pl.when(l == 0)
    def _():
        for h in range(B_HEADS):
            for g in range(G):
                j = h * G + g
                c_scr[j] = c0_ref[g, h]
                n_scr[j] = n0_ref[g, h:h + 1, :]
                m_scr[j * c:(j + 1) * c, :] = jnp.broadcast_to(m0_ref[g, h:h + 1, :], (c, 128))

    args = (qk_ref, v_ref, og_ref, sm_ref, bias_ref, g_ref, y_ref, c_scr, n_scr, m_scr)
    if tl == c:
        _mlstm_chunk(0, c, G, *args)
    else:
        def step(i, carry):
            _mlstm_chunk(pl.multiple_of(i * c, c), c, G, *args)
            return carry
        lax.fori_loop(0, tl // c, step, 0)

    @pl.when(l == pl.num_programs(1) - 1)
    def _():
        for h in range(B_HEADS):
            for g in range(G):
                j = h * G + g
                co_ref[g, h] = c_scr[j]
                no_ref[g, h:h + 1, :] = n_scr[j]
                mo_ref[g, h:h + 1, :] = m_scr[j * c:j * c + 1, 0:1]


def _mlstm(proj, B, L, tl, c, G, bias_row, g, c0, n0, m0, ls):
    assert G == 1 or (L == tl == c and B % G == 0)
    nl = L // tl
    tr = G * tl
    cb = COL_B // 512
    row = lambda b, l: b * nl + l
    return pl.pallas_call(
        functools.partial(_mlstm_body, tl=tl, c=c, G=G),
        grid=(B // G, nl),
        in_specs=[pl.BlockSpec((tr, 512), lambda b, l: (row(b, l), cb)),
                  pl.BlockSpec((tr, 512), lambda b, l: (row(b, l), cb + 1)),
                  pl.BlockSpec((tr, 512), lambda b, l: (row(b, l), cb + 2)),
                  pl.BlockSpec((tr, 128), lambda b, l: (row(b, l), COL_S // 128)),
                  pl.BlockSpec((1, 128), lambda b, l: (0, 0)),
                  pl.BlockSpec((1, 512), lambda b, l: (0, 0)),
                  pl.BlockSpec((None, G, B_HEADS, B_VAL, B_QK), lambda b, l: (ls, b, 0, 0, 0)),
                  pl.BlockSpec((None, G, B_HEADS, B_QK), lambda b, l: (ls, b, 0, 0)),
                  pl.BlockSpec((None, G, B_HEADS, 1), lambda b, l: (ls, b, 0, 0))],
        out_specs=[pl.BlockSpec((tr, 512), lambda b, l: (row(b, l), 0)),
                   pl.BlockSpec((G, B_HEADS, B_VAL, B_QK), lambda b, l: (b, 0, 0, 0)),
                   pl.BlockSpec((G, B_HEADS, B_QK), lambda b, l: (b, 0, 0)),
                   pl.BlockSpec((G, B_HEADS, 1), lambda b, l: (b, 0, 0))],
        out_shape=[jax.ShapeDtypeStruct((B * L, 512), F32),
                   jax.ShapeDtypeStruct((B, B_HEADS, B_VAL, B_QK), F32),
                   jax.ShapeDtypeStruct((B, B_HEADS, B_QK), F32),
                   jax.ShapeDtypeStruct((B, B_HEADS, 1), F32)],
        scratch_shapes=[pltpu.VMEM((B_HEADS * G, B_VAL, B_QK), F32),
                        pltpu.VMEM((B_HEADS * G, 1, B_QK), F32),
                        pltpu.VMEM((B_HEADS * G * c, 128), F32)],
        compiler_params=pltpu.CompilerParams(
            dimension_semantics=("arbitrary", "arbitrary"), vmem_limit_bytes=VMEM_LIMIT),
        name="mlstm",
    )(proj, proj, proj, proj, bias_row, g, c0, n0, m0)


def _unit_lower_inverse_m1(a, c, ri, ci):
    if c <= SUB:
        ad, aoff = a, None
    else:
        ad = jnp.where((ri // SUB) == (ci // SUB), a, 0.0)
        aoff = a - ad
    t = -ad
    p = ad
    n = 2
    while n < min(c, SUB):
        p = _mm(p, p)
        t = t + p + _mm(t, p)
        n *= 2
    if aoff is None:
        return t
    assert c // SUB == 4
    nn = aoff + _mm(t, aoff)
    n2 = _mm(nn, nn)
    x = t + n2 + _mm(n2, t)
    return x - nn - _mm(nn, x)


def _gdn_chunk(r0, c, G, cv_scr, x_ref, sm_ref, arow_ref, dtrow_ref, g_ref, y_ref, s_scr):
    n = G * c
    R = C_HEADS * n
    rows = pl.ds(r0, n)
    rn, cn = _iotas(n)
    seq = (rn // c) == (cn // c)
    sm = sm_ref[rows, :]
    beta_all = jax.nn.sigmoid(sm)
    g_all = arow_ref[...] * _softplus(sm + dtrow_ref[...])
    bcum = _mm_hi((seq & (rn >= cn)).astype(F32), g_all)
    brev = _mm_hi((seq & (rn < cn)).astype(F32), g_all)
    stack = lambda f: jnp.concatenate([f(h) for h in range(C_HEADS)], axis=0)
    q = stack(lambda h: cv_scr[rows, h * 128:(h + 1) * 128])
    k = stack(lambda h: cv_scr[rows, 512 + h * 128:512 + (h + 1) * 128])
    v = stack(lambda h: cv_scr[rows, 1024 + h * 128:1024 + (h + 1) * 128])
    beta = stack(lambda h: beta_all[:, 8 + h:9 + h])
    bc = stack(lambda h: bcum[:, 12 + h:13 + h])
    kdec = jnp.exp(stack(lambda h: brev[:, 12 + h:13 + h]))
    q = q * lax.rsqrt(jnp.sum(q * q, -1, keepdims=True) + NORM_EPS) * (C_KEY ** -0.5)
    k = k * lax.rsqrt(jnp.sum(k * k, -1, keepdims=True) + NORM_EPS)

    ri, ci = _iotas(R)
    grp = (ri // c) == (ci // c)
    incl = grp & (ri >= ci)
    strict = grp & (ri > ci)
    diff = bc - _col2row(bc, ri == ci)
    dec = jnp.exp(jnp.where(incl, diff, 0.0))
    a = beta * _mm_nt(k, k) * jnp.where(strict, dec, 0.0)
    ebc = jnp.exp(bc)
    rhs = jnp.concatenate([v * beta, k * (beta * ebc)], axis=-1)
    sol = rhs + _mm(_unit_lower_inverse_m1(a, c, ri, ci), rhs)
    att = _mm_nt(q, k) * jnp.where(incl, dec, 0.0)
    qe = q * ebc
    kd = k * kdec
    v_new, o_inter, s_new = [], [], []
    s_old = [s_scr[j] for j in range(C_HEADS * G)]
    for j in range(C_HEADS * G):
        sl = slice(j * c, (j + 1) * c)
        s = s_old[j]
        vn = sol[sl, :C_VAL] - _mm(sol[sl, C_VAL:], s)
        o_inter.append(_mm(qe[sl], s))
        hh, gg = divmod(j, G)
        bl = bcum[(gg + 1) * c - 1:(gg + 1) * c, 12 + hh:13 + hh]
        s_new.append(jnp.exp(bl) * s + _mm_tn(kd[sl], vn))
        v_new.append(vn)
    for j in range(C_HEADS * G):
        s_scr[j] = s_new[j]
    o = jnp.concatenate(o_inter, axis=0) + _mm(att, jnp.concatenate(v_new, axis=0))
    cg = x_ref[rows, CONV_CH:2048]
    for h in range(C_HEADS):
        sl = slice(h * 128, (h + 1) * 128)
        y_ref[rows, sl] = _rms(o[h * n:(h + 1) * n]) * g_ref[:, sl] * _silu(cg[:, sl])


def _gdn_body(x_ref, sm_ref, cw_ref, arow_ref, dtrow_ref, g_ref, s0_ref, cs_ref,
              y_ref, so_ref, cso_ref, s_scr, xb_scr, cv_scr, *, tl, c, G):
    l = pl.program_id(1)
    args = (cv_scr, x_ref, sm_ref, arow_ref, dtrow_ref, g_ref, y_ref, s_scr)

    @pl.when(l == 0)
    def _():
        for h in range(C_HEADS):
            for g in range(G):
                s_scr[h * G + g] = s0_ref[g, h]

    if G > 1:
        for g in range(G):
            xb_scr[0:8, :] = cs_ref[g]
            xb_scr[8:8 + c, :] = x_ref[g * c:(g + 1) * c, 0:CONV_CH]
            conv = xb_scr[5:5 + c, :] * cw_ref[0:1, :]
            for j in range(1, CONV_W):
                conv = conv + xb_scr[5 + j:5 + j + c, :] * cw_ref[j:j + 1, :]
            cv_scr[g * c:(g + 1) * c, :] = _silu(conv)
            cso_ref[g] = xb_scr[c + 5:c + 8, :]
        _gdn_chunk(0, c, G, *args)
    else:
        @pl.when(l == 0)
        def _():
            xb_scr[0:8, :] = cs_ref[0]

        xb_scr[8:8 + tl, :] = x_ref[:, 0:CONV_CH]
        conv = xb_scr[5:5 + tl, :] * cw_ref[0:1, :]
        for j in range(1, CONV_W):
            conv = conv + xb_scr[5 + j:5 + j + tl, :] * cw_ref[j:j + 1, :]
        cv_scr[...] = _silu(conv)

        def step(i, carry):
            _gdn_chunk(pl.multiple_of(i * c, c), c, 1, *args)
            return carry
        lax.fori_loop(0, tl // c, step, 0, unroll=2)

        @pl.when(l == pl.num_programs(1) - 1)
        def _():
            cso_ref[0] = xb_scr[tl + 5:tl + 8, :]

        xb_scr[0:8, :] = xb_scr[tl:tl + 8, :]

    @pl.when(l == pl.num_programs(1) - 1)
    def _():
        for h in range(C_HEADS):
            for g in range(G):
                so_ref[g, h] = s_scr[h * G + g]


def _gdn(proj, B, L, tl, c, G, conv_w, arow, dtrow, g, s0, cs_pad, ls):
    assert G == 1 or (L == tl == c and B % G == 0)
    nl = L // tl
    tr = G * tl
    row = lambda b, l: b * nl + l
    return pl.pallas_call(
        functools.partial(_gdn_body, tl=tl, c=c, G=G),
        grid=(B // G, nl),
        in_specs=[pl.BlockSpec((tr, 2048), lambda b, l: (row(b, l), COL_C // 2048)),
                  pl.BlockSpec((tr, 128), lambda b, l: (row(b, l), COL_S // 128)),
                  pl.BlockSpec((CONV_W, CONV_CH), lambda b, l: (0, 0)),
                  pl.BlockSpec((1, 128), lambda b, l: (0, 0)),
                  pl.BlockSpec((1, 128), lambda b, l: (0, 0)),
                  pl.BlockSpec((1, 512), lambda b, l: (0, 0)),
                  pl.BlockSpec((None, G, C_HEADS, C_KEY, C_VAL), lambda b, l: (ls, b, 0, 0, 0)),
                  pl.BlockSpec((None, G, 8, CONV_CH), lambda b, l: (ls, b, 0, 0))],
        out_specs=[pl.BlockSpec((tr, 512), lambda b, l: (row(b, l), 0)),
                   pl.BlockSpec((G, C_HEADS, C_KEY, C_VAL), lambda b, l: (b, 0, 0, 0)),
                   pl.BlockSpec((G, CONV_W - 1, CONV_CH), lambda b, l: (b, 0, 0))],
        out_shape=[jax.ShapeDtypeStruct((B * L, 512), F32),
                   jax.ShapeDtypeStruct((B, C_HEADS, C_KEY, C_VAL), F32),
                   jax.ShapeDtypeStruct((B, CONV_W - 1, CONV_CH), F32)],
        scratch_shapes=[pltpu.VMEM((C_HEADS * G, C_KEY, C_VAL), F32),
                        pltpu.VMEM((tl + 8, CONV_CH), F32),
                        pltpu.VMEM((tr, CONV_CH), F32)],
        compiler_params=pltpu.CompilerParams(
            dimension_semantics=("arbitrary", "arbitrary"), vmem_limit_bytes=VMEM_LIMIT),
        name="gdn",
    )(proj, proj, conv_w, arow, dtrow, g, s0, cs_pad)


def _pad_row(vals, start):
    return jnp.zeros((1, 128), F32).at[0, start:start + vals.shape[0]].set(vals.astype(F32))


def _layer(h, B, L, tl, c, G, st, ls, w, lw, l):
    st_a, st_c, st_n, st_m, st_g, st_conv = st
    proj = _proj(h, w["w_in"], l)
    ya, na = _hgrn(proj, B, L, tl, c, lw["lb"], lw["a_g"], st_a, ls)
    yb, nc, nn, nm = _mlstm(proj, B, L, tl, c, G, lw["b_bias"], lw["b_g"], st_c, st_n, st_m, ls)
    yc, ng, nv = _gdn(proj, B, L, tl, c, G, lw["conv_w"], lw["c_arow"], lw["c_dtrow"], lw["c_g"], st_g, st_conv, ls)
    h = _post(ya, yb, yc, proj, h, w["wb"], w["wo"], w["ln1_g"], w["ln1_b"], w["wfi"], w["wfo"],
              w["ln2_g"], w["ln2_b"], l)
    return h, (na, nc, nn, nm[..., 0], ng, nv)


def kernel(x_prompt, x_sample, state_hgrn, state_mlstm_c, state_mlstm_n, state_mlstm_m, state_gdn,
           state_gdn_conv, w_in, lb_logits, a_norm_g, b_mi, b_mf, b_norm_g, conv_w, a_log, dt_bias,
           c_norm_g, w_branch, w_out, ln1_g, ln1_b, w_ffn_in, w_ffn_out, ln2_g, ln2_b):
    Bp, Lp, _ = x_prompt.shape
    Bs, Ls, _ = x_sample.shape
    cp, cs = math.gcd(Lp, CHUNK), math.gcd(Ls, CHUNK)
    tlp = min(Lp, 256)

    w = dict(w_in=_reorder_w_in(w_in), wb=w_branch.astype(BF16), wo=w_out.astype(BF16),
             wfi=w_ffn_in.astype(BF16), wfo=w_ffn_out.astype(BF16),
             ln1_g=ln1_g[:, None].astype(F32), ln1_b=ln1_b[:, None].astype(F32),
             ln2_g=ln2_g[:, None].astype(F32), ln2_b=ln2_b[:, None].astype(F32))
    lb_w = jax.nn.softmax(lb_logits.astype(F32), axis=0)
    lb_all = jnp.cumsum(lb_w, axis=0) - lb_w[0]
    hist_pad = ((0, 0), (0, 0), (8 - (CONV_W - 1), 0), (0, 0))

    h_p = x_prompt.reshape(Bp * Lp, D_MODEL)
    h_s = x_sample.reshape(Bs * Ls, D_MODEL)
    st_p = (jnp.zeros((1, Bp, A_HEADS, A_KEY, A_VAL), F32), jnp.zeros((1, Bp, B_HEADS, B_VAL, B_QK), F32),
            jnp.zeros((1, Bp, B_HEADS, B_QK), F32), jnp.zeros((1, Bp, B_HEADS, 1), F32),
            jnp.zeros((1, Bp, C_HEADS, C_KEY, C_VAL), F32), jnp.zeros((1, Bp, 8, CONV_CH), F32))
    st_s = (state_hgrn.astype(F32), state_mlstm_c.astype(F32), state_mlstm_n.astype(F32),
            state_mlstm_m.astype(F32)[..., None], state_gdn.astype(F32),
            jnp.pad(state_gdn_conv.astype(F32), hist_pad))
    new_p, new_s = [], []
    for l in range(DEPTH):
        lw = dict(lb=lb_all[l][None], a_g=a_norm_g[l][None].astype(F32),
                  b_bias=_pad_row(jnp.concatenate([b_mi[l], b_mf[l]]), 0), b_g=b_norm_g[l][None].astype(F32),
                  conv_w=conv_w[l].astype(F32), c_arow=_pad_row(-jnp.exp(a_log[l].astype(F32)), 12),
                  c_dtrow=_pad_row(dt_bias[l], 12), c_g=c_norm_g[l][None].astype(F32))
        h_p, ns_p = _layer(h_p, Bp, Lp, tlp, cp, 1, st_p, 0, w, lw, l)
        h_s, ns_s = _layer(h_s, Bs, Ls, Ls, cs, SAMPLE_G, st_s, l, w, lw, l)
        new_p.append(ns_p)
        new_s.append(ns_s)

    outs = [h_p.reshape(Bp, Lp, D_MODEL), h_s.reshape(Bs, Ls, D_MODEL)]
    for i, ref in enumerate((state_hgrn, state_mlstm_c, state_mlstm_n, state_mlstm_m, state_gdn,
                             state_gdn_conv)):
        outs.append(jnp.stack([ns[i] for ns in new_p]).astype(ref.dtype))
        outs.append(jnp.stack([ns[i] for ns in new_s]).astype(ref.dtype))
    return tuple(outs)
```

```python
import functools
import math

import jax
import jax.numpy as jnp
from jax import lax
from jax.experimental import pallas as pl
from jax.experimental.pallas import tpu as pltpu

F32 = jnp.float32
BF16 = jnp.bfloat16
MXU_DT = BF16

D_MODEL = 1024
DEPTH = 4
A_HEADS, A_KEY, A_VAL = 4, 128, 128
B_HEADS, B_QK, B_VAL = 4, 64, 128
C_HEADS, C_KEY, C_VAL = 4, 128, 128
CONV_W = 4
CONV_CH = C_HEADS * (2 * C_KEY + C_VAL)
N_BRANCH = 3
BRANCH_W = 512
FFN_HIDDEN = 2816
CHUNK = 64
ALPHA = (2 * DEPTH) ** 0.25
LN_EPS = 1e-5
NORM_EPS = 1e-6
NEG_BIG = -1e30

COL_A = 0
COL_C = 2048
COL_MG = 4096
COL_B = 7168
COL_S = 8704
N_PROJ = 8832
PROJ_TN = N_PROJ // 3

VMEM_LIMIT = 56 * 1024 * 1024


def _reorder_w_in(w_in):
    a = w_in[..., 0:2048]
    b = w_in[..., 2048:3584]
    s1 = w_in[..., 3584:3592]
    c = w_in[..., 3592:5640]
    s2 = w_in[..., 5640:5648]
    mg = w_in[..., 5648:8720]
    pad = jnp.zeros(w_in.shape[:-1] + (N_PROJ - COL_S - 16,), w_in.dtype)
    return jnp.concatenate([a, c, mg, b, s1, s2, pad], axis=-1).astype(BF16)


def _proj_body(x_ref, w_ref, o_ref):
    o_ref[...] = jnp.dot(x_ref[...].astype(BF16), w_ref[...], preferred_element_type=F32)


def _proj(h, w, l):
    T = h.shape[0]
    tm = min(1024, T)
    return pl.pallas_call(
        _proj_body,
        grid=(T // tm, N_PROJ // PROJ_TN),
        in_specs=[pl.BlockSpec((tm, D_MODEL), lambda i, j: (i, 0)),
                  pl.BlockSpec((None, D_MODEL, PROJ_TN), lambda i, j: (l, 0, j))],
        out_specs=pl.BlockSpec((tm, PROJ_TN), lambda i, j: (i, j)),
        out_shape=jax.ShapeDtypeStruct((T, N_PROJ), F32),
        compiler_params=pltpu.CompilerParams(
            dimension_semantics=("arbitrary", "arbitrary"), vmem_limit_bytes=VMEM_LIMIT),
        name="in_proj",
    )(h, w)


def _ln(x, g, b):
    mu = jnp.mean(x, -1, keepdims=True)
    xc = x - mu
    var = jnp.mean(xc * xc, -1, keepdims=True)
    return xc * lax.rsqrt(var + LN_EPS) * g + b


FFN_HC = 256


def _post_body(ya_ref, yb_ref, yc_ref, mg0_ref, mg1_ref, mg2_ref, h_ref, wb_ref, wo_ref, g1_ref, b1_ref,
               wfi_ref, wfo_ref, g2_ref, b2_ref, o_ref):
    mgs = (mg0_ref, mg1_ref, mg2_ref)
    ys = (ya_ref, yb_ref, yc_ref)
    merged = None
    for n in range(N_BRANCH):
        y = ys[n][...].astype(BF16)
        z = jnp.dot(y, wb_ref[n], preferred_element_type=F32)
        t = jax.nn.sigmoid(mgs[n][...]) * z
        merged = t if merged is None else merged + t
    mix = jnp.dot(merged.astype(BF16), wo_ref[...], preferred_element_type=F32)
    h1 = _ln(ALPHA * h_ref[...] + mix, g1_ref[...], b1_ref[...])
    h1b = h1.astype(BF16)
    ff = None
    for j in range(FFN_HIDDEN // FFN_HC):
        gate = jnp.dot(h1b, wfi_ref[:, j * FFN_HC:(j + 1) * FFN_HC], preferred_element_type=F32)
        up = jnp.dot(h1b, wfi_ref[:, FFN_HIDDEN + j * FFN_HC:FFN_HIDDEN + (j + 1) * FFN_HC],
                     preferred_element_type=F32)
        a = (gate * jax.nn.sigmoid(gate) * up).astype(BF16)
        t = jnp.dot(a, wfo_ref[j * FFN_HC:(j + 1) * FFN_HC, :], preferred_element_type=F32)
        ff = t if ff is None else ff + t
    o_ref[...] = _ln(ALPHA * h1 + ff, g2_ref[...], b2_ref[...])


def _layer_spec(shape, l):
    nd = len(shape)
    return pl.BlockSpec((None,) + shape, lambda i: (l,) + (0,) * nd, pipeline_mode=pl.Buffered(1))


def _post(ya, yb, yc, proj, h, wb, wo, g1, b1, wfi, wfo, g2, b2, l):
    T = h.shape[0]
    tm = min(256, T)
    mg_blk = COL_MG // D_MODEL
    return pl.pallas_call(
        _post_body,
        grid=(T // tm,),
        in_specs=[pl.BlockSpec((tm, BRANCH_W), lambda i: (i, 0)),
                  pl.BlockSpec((tm, BRANCH_W), lambda i: (i, 0)),
                  pl.BlockSpec((tm, BRANCH_W), lambda i: (i, 0)),
                  pl.BlockSpec((tm, D_MODEL), lambda i: (i, mg_blk)),
                  pl.BlockSpec((tm, D_MODEL), lambda i: (i, mg_blk + 1)),
                  pl.BlockSpec((tm, D_MODEL), lambda i: (i, mg_blk + 2)),
                  pl.BlockSpec((tm, D_MODEL), lambda i: (i, 0)),
                  _layer_spec((N_BRANCH, BRANCH_W, D_MODEL), l),
                  _layer_spec((D_MODEL, D_MODEL), l),
                  _layer_spec((1, D_MODEL), l), _layer_spec((1, D_MODEL), l),
                  _layer_spec((D_MODEL, 2 * FFN_HIDDEN), l),
                  _layer_spec((FFN_HIDDEN, D_MODEL), l),
                  _layer_spec((1, D_MODEL), l), _layer_spec((1, D_MODEL), l)],
        out_specs=pl.BlockSpec((tm, D_MODEL), lambda i: (i, 0)),
        out_shape=jax.ShapeDtypeStruct((T, D_MODEL), F32),
        compiler_params=pltpu.CompilerParams(
            dimension_semantics=("arbitrary",), vmem_limit_bytes=VMEM_LIMIT),
        name="merge_ffn",
    )(ya, yb, yc, proj, proj, proj, h, wb, wo, g1, b1, wfi, wfo, g2, b2)


SAMPLE_G = 8
SUB = 16
SUB_A = 16
_HI = lax.Precision.HIGHEST


def _mm(a, b):
    return jnp.dot(a.astype(MXU_DT), b.astype(MXU_DT), preferred_element_type=F32)


def _mm_nt(a, b):
    return lax.dot_general(a.astype(MXU_DT), b.astype(MXU_DT), (((1,), (1,)), ((), ())),
                           preferred_element_type=F32)


def _mm_tn(a, b):
    return lax.dot_general(a.astype(MXU_DT), b.astype(MXU_DT), (((0,), (0,)), ((), ())),
                           preferred_element_type=F32)


def _mm_hi(a, b):
    return jnp.dot(a, b, precision=_HI, preferred_element_type=F32)


def _silu(x):
    return x * jax.nn.sigmoid(x)


def _softplus(x):
    return jnp.maximum(x, 0.0) + jnp.log1p(jnp.exp(-jnp.abs(x)))


def _masked_exp(mask, t):
    return jnp.where(mask, jnp.exp(jnp.where(mask, t, 0.0)), 0.0)


def _rms(o):
    return o * lax.rsqrt(jnp.mean(o * o, -1, keepdims=True) + NORM_EPS)


def _col2row(col, eye):
    return jnp.sum(jnp.where(eye, col, 0.0), axis=0, keepdims=True)


def _iotas(c):
    ri = lax.broadcasted_iota(jnp.int32, (c, c), 0)
    ci = lax.broadcasted_iota(jnp.int32, (c, c), 1)
    return ri, ci


def _hgrn_chunk_pre(r0, c, a_ref, lb_ref):
    rows = pl.ds(r0, c)
    ri, ci = _iotas(c)
    tri = (ri >= ci).astype(F32)
    lb = lb_ref[...]
    af = a_ref[rows, 512:1024]
    logf = jnp.log(lb + (1.0 - lb) * jax.nn.sigmoid(af))
    kk = (1.0 - lb) * jax.nn.sigmoid(-af)
    b = _mm_hi(tri, logf)
    q = _silu(a_ref[rows, 0:512])
    v = a_ref[rows, 1024:1536]
    sb = min(SUB_A, c)
    trow = lax.broadcasted_iota(jnp.int32, (sb, 1), 0)
    intra = []
    for h in range(A_HEADS):
        sl = slice(h * 128, (h + 1) * 128)
        qh, kh, vh, bh = q[:, sl], kk[:, sl], v[:, sl], b[:, sl]
        blocks = []
        for i in range(c // sb):
            rs = slice(i * sb, (i + 1) * sb)
            qi, bi = qh[rs], bh[rs]
            oi = jnp.zeros((sb, A_VAL), F32)
            if i > 0:
                r = bh[i * sb - 1:i * sb]
                att = _mm_nt(qi * jnp.exp(bi - r), kh[:i * sb] * jnp.exp(r - bh[:i * sb]))
                oi = _mm(att, vh[:i * sb])
            ki, vi = kh[rs], vh[rs]
            for s in range(sb):
                msk = trow >= s
                p = _masked_exp(msk, bi - bi[s:s + 1]) * qi * ki[s:s + 1]
                oi = oi + jnp.sum(p, axis=-1, keepdims=True) * vi[s:s + 1]
            blocks.append(oi)
        intra.append(blocks[0] if len(blocks) == 1 else jnp.concatenate(blocks, axis=0))
    bl = b[c - 1:c]
    return intra, q * jnp.exp(b), kk * jnp.exp(bl - b), v, jnp.exp(bl)


def _hgrn_chunk_post(r0, c, pre, a_ref, g_ref, y_ref, st_scr):
    intra, qe, kd, v, ebl = pre
    rows = pl.ds(r0, c)
    ag = a_ref[rows, 1536:2048]
    for h in range(A_HEADS):
        sl = slice(h * 128, (h + 1) * 128)
        st = st_scr[h]
        o = intra[h] + _mm_nt(qe[:, sl], st)
        st_scr[h] = st * ebl[:, sl] + _mm_tn(v[:, sl], kd[:, sl])
        y_ref[rows, sl] = _rms(o) * g_ref[:, sl] * _silu(ag[:, sl])


def _hgrn_body(a_ref, lb_ref, g_ref, s0_ref, y_ref, so_ref, st_scr, *, c, nchunks):
    l = pl.program_id(1)

    @pl.when(l == 0)
    def _():
        for h in range(A_HEADS):
            st_scr[h] = s0_ref[0, h].T

    r0s = [i * c for i in range(nchunks)]
    pre = [_hgrn_chunk_pre(r0, c, a_ref, lb_ref) for r0 in r0s]
    for r0, p in zip(r0s, pre):
        _hgrn_chunk_post(r0, c, p, a_ref, g_ref, y_ref, st_scr)

    @pl.when(l == pl.num_programs(1) - 1)
    def _():
        for h in range(A_HEADS):
            so_ref[0, h] = st_scr[h].T


def _hgrn(proj, B, L, tl, c, lb, g, s0, ls):
    nl = L // tl
    return pl.pallas_call(
        functools.partial(_hgrn_body, c=c, nchunks=tl // c),
        grid=(B, nl),
        in_specs=[pl.BlockSpec((tl, 2048), lambda b, l: (b * nl + l, COL_A // 2048)),
                  pl.BlockSpec((1, 512), lambda b, l: (0, 0)),
                  pl.BlockSpec((1, 512), lambda b, l: (0, 0)),
                  pl.BlockSpec((None, 1, A_HEADS, A_KEY, A_VAL), lambda b, l: (ls, b, 0, 0, 0))],
        out_specs=[pl.BlockSpec((tl, 512), lambda b, l: (b * nl + l, 0)),
                   pl.BlockSpec((1, A_HEADS, A_KEY, A_VAL), lambda b, l: (b, 0, 0, 0))],
        out_shape=[jax.ShapeDtypeStruct((B * L, 512), F32),
                   jax.ShapeDtypeStruct((B, A_HEADS, A_KEY, A_VAL), F32)],
        scratch_shapes=[pltpu.VMEM((A_HEADS, A_VAL, A_KEY), F32)],
        compiler_params=pltpu.CompilerParams(
            dimension_semantics=("arbitrary", "arbitrary"), vmem_limit_bytes=VMEM_LIMIT),
        name="hgrn2",
    )(proj, lb, g, s0)


def _mlstm_chunk(r0, c, G, qk_ref, v_ref, og_ref, sm_ref, bias_ref, g_ref, y_ref, c_scr, n_scr, m_scr):
    n = G * c
    R = B_HEADS * n
    rows = pl.ds(r0, n)
    rn, cn = _iotas(n)
    seq = (rn // c) == (cn // c)
    pre = sm_ref[rows, :] + bias_ref[...]
    lf = -_softplus(-pre)
    bcum = _mm_hi((seq & (rn >= cn)).astype(F32), lf)
    brev = _mm_hi((seq & (rn < cn)).astype(F32), lf)
    stack = lambda f: jnp.concatenate([f(h) for h in range(B_HEADS)], axis=0)
    q = stack(lambda h: qk_ref[rows, h * 64:(h + 1) * 64]) * (B_QK ** -0.5)
    k = stack(lambda h: qk_ref[rows, 256 + h * 64:256 + (h + 1) * 64])
    v = stack(lambda h: v_ref[rows, h * 128:(h + 1) * 128])
    ic = stack(lambda h: pre[:, h:h + 1])
    bc = stack(lambda h: bcum[:, 4 + h:5 + h])
    gg = stack(lambda h: brev[:, 4 + h:5 + h]) + ic
    m_st = m_scr[:, 0:1]

    qc, qn, wk = [], [], []
    for j in range(B_HEADS * G):
        sl = slice(j * c, (j + 1) * c)
        hh, g_ = divmod(j, G)
        cm = c_scr[j]
        nj = n_scr[j]
        m = m_scr[j * c:j * c + 1, 0:1]
        qc.append(_mm_nt(q[sl], cm))
        qn.append(jnp.sum(q[sl] * nj, -1, keepdims=True))
        bl = bcum[(g_ + 1) * c - 1:(g_ + 1) * c, 4 + hh:5 + hh]
        m_new = jnp.maximum(bl + m, jnp.max(gg[sl], 0, keepdims=True))
        w_old = jnp.exp(bl + m - m_new)
        wkj = jnp.exp(gg[sl] - m_new)
        c_scr[j] = w_old * cm + _mm_tn(v[sl] * wkj, k[sl])
        n_scr[j] = w_old * nj + jnp.sum(wkj * k[sl], 0, keepdims=True)
        m_scr[sl, :] = jnp.broadcast_to(m_new, (c, 128))
    qc = jnp.concatenate(qc, axis=0)
    qn = jnp.concatenate(qn, axis=0)

    ri, ci = _iotas(R)
    incl = ((ri // c) == (ci // c)) & (ri >= ci)
    eye = ri == ci
    logd = jnp.where(incl, bc - _col2row(bc, eye) + _col2row(ic, eye), NEG_BIG)
    m_inter = bc + m_st
    m_t = jnp.maximum(m_inter, jnp.max(logd, -1, keepdims=True))
    w_inter = jnp.exp(m_inter - m_t)
    sc = _mm_nt(q, k) * _masked_exp(incl, logd - m_t)
    num = w_inter * qc + _mm(sc, v)
    den = w_inter * qn + jnp.sum(sc, -1, keepdims=True)
    hh = num / jnp.maximum(jnp.abs(den), jnp.exp(-m_t))
    for h in range(B_HEADS):
        sl = slice(h * 128, (h + 1) * 128)
        y_ref[rows, sl] = _rms(hh[h * n:(h + 1) * n]) * g_ref[:, sl] * jax.nn.sigmoid(og_ref[rows, sl])


def _mlstm_body(qk_ref, v_ref, og_ref, sm_ref, bias_ref, g_ref, c0_ref, n0_ref, m0_ref,
                y_ref, co_ref, no_ref, mo_ref, c_scr, n_scr, m_scr, *, tl, c, G):
    l = pl.program_id(1)

    @pl.when(l == 0)
    def _():
        for h in range(B_HEADS):
            for g in range(G):
                j = h * G + g
                c_scr[j] = c0_ref[g, h]
                n_scr[j] = n0_ref[g, h:h + 1, :]
                m_scr[j * c:(j + 1) * c, :] = jnp.broadcast_to(m0_ref[g, h:h + 1, :], (c, 128))

    args = (qk_ref, v_ref, og_ref, sm_ref, bias_ref, g_ref, y_ref, c_scr, n_scr, m_scr)
    if tl == c:
        _mlstm_chunk(0, c, G, *args)
    else:
        def step(i, carry):
            _mlstm_chunk(pl.multiple_of(i * c, c), c, G, *args)
            return carry
        lax.fori_loop(0, tl // c, step, 0)

    @pl.when(l == pl.num_programs(1) - 1)
    def _():
        for h in range(B_HEADS):
            for g in range(G):
                j = h * G + g
                co_ref[g, h] = c_scr[j]
                no_ref[g, h:h + 1, :] = n_scr[j]
                mo_ref[g, h:h + 1, :] = m_scr[j * c:j * c + 1, 0:1]


def _mlstm(proj, B, L, tl, c, G, bias_row, g, c0, n0, m0, ls):
    assert G == 1 or (L == tl == c and B % G == 0)
    nl = L // tl
    tr = G * tl
    cb = COL_B // 512
    row = lambda b, l: b * nl + l
    return pl.pallas_call(
        functools.partial(_mlstm_body, tl=tl, c=c, G=G),
        grid=(B // G, nl),
        in_specs=[pl.BlockSpec((tr, 512), lambda b, l: (row(b, l), cb)),
                  pl.BlockSpec((tr, 512), lambda b, l: (row(b, l), cb + 1)),
                  pl.BlockSpec((tr, 512), lambda b, l: (row(b, l), cb + 2)),
                  pl.BlockSpec((tr, 128), lambda b, l: (row(b, l), COL_S // 128)),
                  pl.BlockSpec((1, 128), lambda b, l: (0, 0)),
                  pl.BlockSpec((1, 512), lambda b, l: (0, 0)),
                  pl.BlockSpec((None, G, B_HEADS, B_VAL, B_QK), lambda b, l: (ls, b, 0, 0, 0)),
                  pl.BlockSpec((None, G, B_HEADS, B_QK), lambda b, l: (ls, b, 0, 0)),
                  pl.BlockSpec((None, G, B_HEADS, 1), lambda b, l: (ls, b, 0, 0))],
        out_specs=[pl.BlockSpec((tr, 512), lambda b, l: (row(b, l), 0)),
                   pl.BlockSpec((G, B_HEADS, B_VAL, B_QK), lambda b, l: (b, 0, 0, 0)),
                   pl.BlockSpec((G, B_HEADS, B_QK), lambda b, l: (b, 0, 0)),
                   pl.BlockSpec((G, B_HEADS, 1), lambda b, l: (b, 0, 0))],
        out_shape=[jax.ShapeDtypeStruct((B * L, 512), F32),
                   jax.ShapeDtypeStruct((B, B_HEADS, B_VAL, B_QK), F32),
                   jax.ShapeDtypeStruct((B, B_HEADS, B_QK), F32),
                   jax.ShapeDtypeStruct((B, B_HEADS, 1), F32)],
        scratch_shapes=[pltpu.VMEM((B_HEADS * G, B_VAL, B_QK), F32),
                        pltpu.VMEM((B_HEADS * G, 1, B_QK), F32),
                        pltpu.VMEM((B_HEADS * G * c, 128), F32)],
        compiler_params=pltpu.CompilerParams(
            dimension_semantics=("arbitrary", "arbitrary"), vmem_limit_bytes=VMEM_LIMIT),
        name="mlstm",
    )(proj, proj, proj, proj, bias_row, g, c0, n0, m0)


def _unit_lower_inverse_m1(mats, c, ri, ci):
    if c <= SUB:
        ads, aoffs = mats, None
    else:
        same = (ri // SUB) == (ci // SUB)
        ads = [jnp.where(same, a, 0.0) for a in mats]
        aoffs = [a - ad for a, ad in zip(mats, ads)]
    ts = [-ad for ad in ads]
    ps = ads
    n = 2
    while n < min(c, SUB):
        ps = [_mm(p, p) for p in ps]
        ts = [t + p + _mm(t, p) for t, p in zip(ts, ps)]
        n *= 2
    if aoffs is None:
        return ts
    assert c // SUB == 4
    nns = [aoff + _mm(t, aoff) for t, aoff in zip(ts, aoffs)]
    n2s = [_mm(nn, nn) for nn in nns]
    xs = [t + n2 + _mm(n2, t) for t, n2 in zip(ts, n2s)]
    return [x - nn - _mm(nn, x) for x, nn in zip(xs, nns)]


def _gdn_chunks(r0s, c, G, cv_scr, x_ref, sm_ref, arow_ref, dtrow_ref, g_ref, y_ref, s_scr):
    pre = [_gdn_chunk_pre(r0, c, G, cv_scr, sm_ref, arow_ref, dtrow_ref) for r0 in r0s]
    ri, ci = _iotas(C_HEADS * G * c)
    invs = _unit_lower_inverse_m1([p[0] for p in pre], c, ri, ci)
    for r0, p, u in zip(r0s, pre, invs):
        _gdn_chunk_post(r0, c, G, p, u, x_ref, g_ref, y_ref, s_scr)


def _gdn_chunk_pre(r0, c, G, cv_scr, sm_ref, arow_ref, dtrow_ref):
    n = G * c
    R = C_HEADS * n
    rows = pl.ds(r0, n)
    rn, cn = _iotas(n)
    seq = (rn // c) == (cn // c)
    sm = sm_ref[rows, :]
    beta_all = jax.nn.sigmoid(sm)
    g_all = arow_ref[...] * _softplus(sm + dtrow_ref[...])
    bcum = _mm_hi((seq & (rn >= cn)).astype(F32), g_all)
    brev = _mm_hi((seq & (rn < cn)).astype(F32), g_all)
    stack = lambda f: jnp.concatenate([f(h) for h in range(C_HEADS)], axis=0)
    q = stack(lambda h: cv_scr[rows, h * 128:(h + 1) * 128])
    k = stack(lambda h: cv_scr[rows, 512 + h * 128:512 + (h + 1) * 128])
    v = stack(lambda h: cv_scr[rows, 1024 + h * 128:1024 + (h + 1) * 128])
    beta = stack(lambda h: beta_all[:, 8 + h:9 + h])
    bc = stack(lambda h: bcum[:, 12 + h:13 + h])
    kdec = jnp.exp(stack(lambda h: brev[:, 12 + h:13 + h]))
    q = q * lax.rsqrt(jnp.sum(q * q, -1, keepdims=True) + NORM_EPS) * (C_KEY ** -0.5)
    k = k * lax.rsqrt(jnp.sum(k * k, -1, keepdims=True) + NORM_EPS)

    ri, ci = _iotas(R)
    grp = (ri // c) == (ci // c)
    incl = grp & (ri >= ci)
    strict = grp & (ri > ci)
    diff = bc - _col2row(bc, ri == ci)
    dec = jnp.exp(jnp.where(incl, diff, 0.0))
    a = beta * _mm_nt(k, k) * jnp.where(strict, dec, 0.0)
    ebc = jnp.exp(bc)
    rhs = jnp.concatenate([v * beta, k * (beta * ebc)], axis=-1)
    att = _mm_nt(q, k) * jnp.where(incl, dec, 0.0)
    return a, rhs, att, q * ebc, k * kdec, bcum


def _gdn_chunk_post(r0, c, G, pre, inv_m1, x_ref, g_ref, y_ref, s_scr):
    _, rhs, att, qe, kd, bcum = pre
    n = G * c
    rows = pl.ds(r0, n)
    sol = rhs + _mm(inv_m1, rhs)
    v_new, o_inter, s_new = [], [], []
    s_old = [s_scr[j] for j in range(C_HEADS * G)]
    for j in range(C_HEADS * G):
        sl = slice(j * c, (j + 1) * c)
        s = s_old[j]
        vn = sol[sl, :C_VAL] - _mm(sol[sl, C_VAL:], s)
        o_inter.append(_mm(qe[sl], s))
        hh, gg = divmod(j, G)
        bl = bcum[(gg + 1) * c - 1:(gg + 1) * c, 12 + hh:13 + hh]
        s_new.append(jnp.exp(bl) * s + _mm_tn(kd[sl], vn))
        v_new.append(vn)
    for j in range(C_HEADS * G):
        s_scr[j] = s_new[j]
    o = jnp.concatenate(o_inter, axis=0) + _mm(att, jnp.concatenate(v_new, axis=0))
    cg = x_ref[rows, CONV_CH:2048]
    for h in range(C_HEADS):
        sl = slice(h * 128, (h + 1) * 128)
        y_ref[rows, sl] = _rms(o[h * n:(h + 1) * n]) * g_ref[:, sl] * _silu(cg[:, sl])


def _gdn_body(x_ref, sm_ref, cw_ref, arow_ref, dtrow_ref, g_ref, s0_ref, cs_ref,
              y_ref, so_ref, cso_ref, s_scr, xb_scr, cv_scr, *, tl, c, G):
    l = pl.program_id(1)
    args = (cv_scr, x_ref, sm_ref, arow_ref, dtrow_ref, g_ref, y_ref, s_scr)

    @pl.when(l == 0)
    def _():
        for h in range(C_HEADS):
            for g in range(G):
                s_scr[h * G + g] = s0_ref[g, h]

    if G > 1:
        for g in range(G):
            xb_scr[0:8, :] = cs_ref[g]
            xb_scr[8:8 + c, :] = x_ref[g * c:(g + 1) * c, 0:CONV_CH]
            conv = xb_scr[5:5 + c, :] * cw_ref[0:1, :]
            for j in range(1, CONV_W):
                conv = conv + xb_scr[5 + j:5 + j + c, :] * cw_ref[j:j + 1, :]
            cv_scr[g * c:(g + 1) * c, :] = _silu(conv)
            cso_ref[g] = xb_scr[c + 5:c + 8, :]
        _gdn_chunks([0], c, G, *args)
    else:
        @pl.when(l == 0)
        def _():
            xb_scr[0:8, :] = cs_ref[0]

        xb_scr[8:8 + tl, :] = x_ref[:, 0:CONV_CH]
        conv = xb_scr[5:5 + tl, :] * cw_ref[0:1, :]
        for j in range(1, CONV_W):
            conv = conv + xb_scr[5 + j:5 + j + tl, :] * cw_ref[j:j + 1, :]
        cv_scr[...] = _silu(conv)

        _gdn_chunks([i * c for i in range(tl // c)], c, 1, *args)

        @pl.when(l == pl.num_programs(1) - 1)
        def _():
            cso_ref[0] = xb_scr[tl + 5:tl + 8, :]

        xb_scr[0:8, :] = xb_scr[tl:tl + 8, :]

    @pl.when(l == pl.num_programs(1) - 1)
    def _():
        for h in range(C_HEADS):
            for g in range(G):
                so_ref[g, h] = s_scr[h * G + g]


def _gdn(proj, B, L, tl, c, G, conv_w, arow, dtrow, g, s0, cs_pad, ls):
    assert G == 1 or (L == tl == c and B % G == 0)
    nl = L // tl
    tr = G * tl
    row = lambda b, l: b * nl + l
    return pl.pallas_call(
        functools.partial(_gdn_body, tl=tl, c=c, G=G),
        grid=(B // G, nl),
        in_specs=[pl.BlockSpec((tr, 2048), lambda b, l: (row(b, l), COL_C // 2048)),
                  pl.BlockSpec((tr, 128), lambda b, l: (row(b, l), COL_S // 128)),
                  pl.BlockSpec((CONV_W, CONV_CH), lambda b, l: (0, 0)),
                  pl.BlockSpec((1, 128), lambda b, l: (0, 0)),
                  pl.BlockSpec((1, 128), lambda b, l: (0, 0)),
                  pl.BlockSpec((1, 512), lambda b, l: (0, 0)),
                  pl.BlockSpec((None, G, C_HEADS, C_KEY, C_VAL), lambda b, l: (ls, b, 0, 0, 0)),
                  pl.BlockSpec((None, G, 8, CONV_CH), lambda b, l: (ls, b, 0, 0))],
        out_specs=[pl.BlockSpec((tr, 512), lambda b, l: (row(b, l), 0)),
                   pl.BlockSpec((G, C_HEADS, C_KEY, C_VAL), lambda b, l: (b, 0, 0, 0)),
                   pl.BlockSpec((G, CONV_W - 1, CONV_CH), lambda b, l: (b, 0, 0))],
        out_shape=[jax.ShapeDtypeStruct((B * L, 512), F32),
                   jax.ShapeDtypeStruct((B, C_HEADS, C_KEY, C_VAL), F32),
                   jax.ShapeDtypeStruct((B, CONV_W - 1, CONV_CH), F32)],
        scratch_shapes=[pltpu.VMEM((C_HEADS * G, C_KEY, C_VAL), F32),
                        pltpu.VMEM((tl + 8, CONV_CH), F32),
                        pltpu.VMEM((tr, CONV_CH), F32)],
        compiler_params=pltpu.CompilerParams(
            dimension_semantics=("arbitrary", "arbitrary"), vmem_limit_bytes=VMEM_LIMIT),
        name="gdn",
    )(proj, proj, conv_w, arow, dtrow, g, s0, cs_pad)


def _pad_row(vals, start):
    return jnp.zeros((1, 128), F32).at[0, start:start + vals.shape[0]].set(vals.astype(F32))


def _layer(h, B, L, tl, c, G, st, ls, w, lw, l):
    st_a, st_c, st_n, st_m, st_g, st_conv = st
    proj = _proj(h, w["w_in"], l)
    ya, na = _hgrn(proj, B, L, tl, c, lw["lb"], lw["a_g"], st_a, ls)
    yb, nc, nn, nm = _mlstm(proj, B, L, tl, c, G, lw["b_bias"], lw["b_g"], st_c, st_n, st_m, ls)
    yc, ng, nv = _gdn(proj, B, L, tl, c, G, lw["conv_w"], lw["c_arow"], lw["c_dtrow"], lw["c_g"], st_g, st_conv, ls)
    h = _post(ya, yb, yc, proj, h, w["wb"], w["wo"], w["ln1_g"], w["ln1_b"], w["wfi"], w["wfo"],
              w["ln2_g"], w["ln2_b"], l)
    return h, (na, nc, nn, nm[..., 0], ng, nv)


def kernel(x_prompt, x_sample, state_hgrn, state_mlstm_c, state_mlstm_n, state_mlstm_m, state_gdn,
           state_gdn_conv, w_in, lb_logits, a_norm_g, b_mi, b_mf, b_norm_g, conv_w, a_log, dt_bias,
           c_norm_g, w_branch, w_out, ln1_g, ln1_b, w_ffn_in, w_ffn_out, ln2_g, ln2_b):
    Bp, Lp, _ = x_prompt.shape
    Bs, Ls, _ = x_sample.shape
    cp, cs = math.gcd(Lp, CHUNK), math.gcd(Ls, CHUNK)
    tlp = min(Lp, 256)

    w = dict(w_in=_reorder_w_in(w_in), wb=w_branch.astype(BF16), wo=w_out.astype(BF16),
             wfi=w_ffn_in.astype(BF16), wfo=w_ffn_out.astype(BF16),
             ln1_g=ln1_g[:, None].astype(F32), ln1_b=ln1_b[:, None].astype(F32),
             ln2_g=ln2_g[:, None].astype(F32), ln2_b=ln2_b[:, None].astype(F32))
    lb_w = jax.nn.softmax(lb_logits.astype(F32), axis=0)
    lb_all = jnp.cumsum(lb_w, axis=0) - lb_w[0]
    hist_pad = ((0, 0), (0, 0), (8 - (CONV_W - 1), 0), (0, 0))

    h_p = x_prompt.reshape(Bp * Lp, D_MODEL)
    h_s = x_sample.reshape(Bs * Ls, D_MODEL)
    st_p = (jnp.zeros((1, Bp, A_HEADS, A_KEY, A_VAL), F32), jnp.zeros((1, Bp, B_HEADS, B_VAL, B_QK), F32),
            jnp.zeros((1, Bp, B_HEADS, B_QK), F32), jnp.zeros((1, Bp, B_HEADS, 1), F32),
            jnp.zeros((1, Bp, C_HEADS, C_KEY, C_VAL), F32), jnp.zeros((1, Bp, 8, CONV_CH), F32))
    st_s = (state_hgrn.astype(F32), state_mlstm_c.astype(F32), state_mlstm_n.astype(F32),
            state_mlstm_m.astype(F32)[..., None], state_gdn.astype(F32),
            jnp.pad(state_gdn_conv.astype(F32), hist_pad))
    new_p, new_s = [], []
    for l in range(DEPTH):
        lw = dict(lb=lb_all[l][None], a_g=a_norm_g[l][None].astype(F32),
                  b_bias=_pad_row(jnp.concatenate([b_mi[l], b_mf[l]]), 0), b_g=b_norm_g[l][None].astype(F32),
                  conv_w=conv_w[l].astype(F32), c_arow=_pad_row(-jnp.exp(a_log[l].astype(F32)), 12),
                  c_dtrow=_pad_row(dt_bias[l], 12), c_g=c_norm_g[l][None].astype(F32))
        h_p, ns_p = _layer(h_p, Bp, Lp, tlp, cp, 1, st_p, 0, w, lw, l)
        h_s, ns_s = _layer(h_s, Bs, Ls, Ls, cs, SAMPLE_G, st_s, l, w, lw, l)
        new_p.append(ns_p)
        new_s.append(ns_s)

    outs = [h_p.reshape(Bp, Lp, D_MODEL), h_s.reshape(Bs, Ls, D_MODEL)]
    for i, ref in enumerate((state_hgrn, state_mlstm_c, state_mlstm_n, state_mlstm_m, state_gdn,
                             state_gdn_conv)):
        outs.append(jnp.stack([ns[i] for ns in new_p]).astype(ref.dtype))
        outs.append(jnp.stack([ns[i] for ns in new_s]).astype(ref.dtype))
    return tuple(outs)
```

```python
import functools
import math

import jax
import jax.numpy as jnp
from jax import lax
from jax.experimental import pallas as pl
from jax.experimental.pallas import tpu as pltpu

F32 = jnp.float32
BF16 = jnp.bfloat16
MXU_DT = BF16

D_MODEL = 1024
DEPTH = 4
A_HEADS, A_KEY, A_VAL = 4, 128, 128
B_HEADS, B_QK, B_VAL = 4, 64, 128
C_HEADS, C_KEY, C_VAL = 4, 128, 128
CONV_W = 4
CONV_CH = C_HEADS * (2 * C_KEY + C_VAL)
N_BRANCH = 3
BRANCH_W = 512
FFN_HIDDEN = 2816
CHUNK = 64
ALPHA = (2 * DEPTH) ** 0.25
LN_EPS = 1e-5
NORM_EPS = 1e-6
NEG_BIG = -1e30

COL_A = 0
COL_C = 2048
COL_MG = 4096
COL_B = 7168
COL_S = 8704
N_PROJ = 8832
PROJ_TN = N_PROJ // 3

VMEM_LIMIT = 56 * 1024 * 1024


def _reorder_w_in(w_in):
    a = w_in[..., 0:2048]
    b = w_in[..., 2048:3584]
    s1 = w_in[..., 3584:3592]
    c = w_in[..., 3592:5640]
    s2 = w_in[..., 5640:5648]
    mg = w_in[..., 5648:8720]
    pad = jnp.zeros(w_in.shape[:-1] + (N_PROJ - COL_S - 16,), w_in.dtype)
    return jnp.concatenate([a, c, mg, b, s1, s2, pad], axis=-1).astype(BF16)


def _proj_body(x_ref, w_ref, o_ref):
    o_ref[...] = jnp.dot(x_ref[...].astype(BF16), w_ref[...], preferred_element_type=F32)


def _proj(h, w, l):
    T = h.shape[0]
    tm = min(1024, T)
    return pl.pallas_call(
        _proj_body,
        grid=(T // tm, N_PROJ // PROJ_TN),
        in_specs=[pl.BlockSpec((tm, D_MODEL), lambda i, j: (i, 0)),
                  pl.BlockSpec((None, D_MODEL, PROJ_TN), lambda i, j: (l, 0, j))],
        out_specs=pl.BlockSpec((tm, PROJ_TN), lambda i, j: (i, j)),
        out_shape=jax.ShapeDtypeStruct((T, N_PROJ), F32),
        compiler_params=pltpu.CompilerParams(
            dimension_semantics=("arbitrary", "arbitrary"), vmem_limit_bytes=VMEM_LIMIT),
        name="in_proj",
    )(h, w)


def _ln(x, g, b):
    mu = jnp.mean(x, -1, keepdims=True)
    xc = x - mu
    var = jnp.mean(xc * xc, -1, keepdims=True)
    return xc * lax.rsqrt(var + LN_EPS) * g + b


FFN_HC = 256


def _post_body(ya_ref, yb_ref, yc_ref, mg0_ref, mg1_ref, mg2_ref, h_ref, wb_ref, wo_ref, g1_ref, b1_ref,
               wfi_ref, wfo_ref, g2_ref, b2_ref, o_ref, a_scr):
    mgs = (mg0_ref, mg1_ref, mg2_ref)
    ys = (ya_ref, yb_ref, yc_ref)
    merged = None
    for n in range(N_BRANCH):
        y = ys[n][...].astype(BF16)
        z = jnp.dot(y, wb_ref[n], preferred_element_type=F32)
        t = jax.nn.sigmoid(mgs[n][...]) * z
        merged = t if merged is None else merged + t
    mix = jnp.dot(merged.astype(BF16), wo_ref[...], preferred_element_type=F32)
    h1 = _ln(ALPHA * h_ref[...] + mix, g1_ref[...], b1_ref[...])
    h1b = h1.astype(BF16)
    for j in range(FFN_HIDDEN // FFN_HC):
        gate = jnp.dot(h1b, wfi_ref[:, j * FFN_HC:(j + 1) * FFN_HC], preferred_element_type=F32)
        up = jnp.dot(h1b, wfi_ref[:, FFN_HIDDEN + j * FFN_HC:FFN_HIDDEN + (j + 1) * FFN_HC],
                     preferred_element_type=F32)
        a_scr[:, j * FFN_HC:(j + 1) * FFN_HC] = (gate * jax.nn.sigmoid(gate) * up).astype(BF16)
    ff = jnp.dot(a_scr[...], wfo_ref[...], preferred_element_type=F32)
    o_ref[...] = _ln(ALPHA * h1 + ff, g2_ref[...], b2_ref[...])


def _layer_spec(shape, l):
    nd = len(shape)
    return pl.BlockSpec((None,) + shape, lambda i: (l,) + (0,) * nd, pipeline_mode=pl.Buffered(1))


def _post(ya, yb, yc, proj, h, wb, wo, g1, b1, wfi, wfo, g2, b2, l):
    T = h.shape[0]
    tm = min(256, T)
    mg_blk = COL_MG // D_MODEL
    return pl.pallas_call(
        _post_body,
        grid=(T // tm,),
        in_specs=[pl.BlockSpec((tm, BRANCH_W), lambda i: (i, 0)),
                  pl.BlockSpec((tm, BRANCH_W), lambda i: (i, 0)),
                  pl.BlockSpec((tm, BRANCH_W), lambda i: (i, 0)),
                  pl.BlockSpec((tm, D_MODEL), lambda i: (i, mg_blk)),
                  pl.BlockSpec((tm, D_MODEL), lambda i: (i, mg_blk + 1)),
                  pl.BlockSpec((tm, D_MODEL), lambda i: (i, mg_blk + 2)),
                  pl.BlockSpec((tm, D_MODEL), lambda i: (i, 0)),
                  _layer_spec((N_BRANCH, BRANCH_W, D_MODEL), l),
                  _layer_spec((D_MODEL, D_MODEL), l),
                  _layer_spec((1, D_MODEL), l), _layer_spec((1, D_MODEL), l),
                  _layer_spec((D_MODEL, 2 * FFN_HIDDEN), l),
                  _layer_spec((FFN_HIDDEN, D_MODEL), l),
                  _layer_spec((1, D_MODEL), l), _layer_spec((1, D_MODEL), l)],
        out_specs=pl.BlockSpec((tm, D_MODEL), lambda i: (i, 0)),
        out_shape=jax.ShapeDtypeStruct((T, D_MODEL), F32),
        scratch_shapes=[pltpu.VMEM((tm, FFN_HIDDEN), BF16)],
        compiler_params=pltpu.CompilerParams(
            dimension_semantics=("arbitrary",), vmem_limit_bytes=VMEM_LIMIT),
        name="merge_ffn",
    )(ya, yb, yc, proj, proj, proj, h, wb, wo, g1, b1, wfi, wfo, g2, b2)


SAMPLE_G = 8
SUB = 16
SUB_A = 16
_HI = lax.Precision.HIGHEST


def _mm(a, b):
    return jnp.dot(a.astype(MXU_DT), b.astype(MXU_DT), preferred_element_type=F32)


def _mm_nt(a, b):
    return lax.dot_general(a.astype(MXU_DT), b.astype(MXU_DT), (((1,), (1,)), ((), ())),
                           preferred_element_type=F32)


def _mm_tn(a, b):
    return lax.dot_general(a.astype(MXU_DT), b.astype(MXU_DT), (((0,), (0,)), ((), ())),
                           preferred_element_type=F32)


def _mm_hi(a, b):
    return jnp.dot(a, b, precision=_HI, preferred_element_type=F32)


def _silu(x):
    return x * jax.nn.sigmoid(x)


def _softplus(x):
    return jnp.maximum(x, 0.0) + jnp.log1p(jnp.exp(-jnp.abs(x)))


def _masked_exp(mask, t):
    return jnp.where(mask, jnp.exp(jnp.where(mask, t, 0.0)), 0.0)


def _rms(o):
    return o * lax.rsqrt(jnp.mean(o * o, -1, keepdims=True) + NORM_EPS)


def _col2row(col, eye):
    return jnp.sum(jnp.where(eye, col, 0.0), axis=0, keepdims=True)


def _iotas(c):
    ri = lax.broadcasted_iota(jnp.int32, (c, c), 0)
    ci = lax.broadcasted_iota(jnp.int32, (c, c), 1)
    return ri, ci


def _hgrn_chunk_pre(r0, c, a_ref, lb_ref):
    rows = pl.ds(r0, c)
    ri, ci = _iotas(c)
    tri = (ri >= ci).astype(F32)
    lb = lb_ref[...]
    af = a_ref[rows, 512:1024]
    logf = jnp.log(lb + (1.0 - lb) * jax.nn.sigmoid(af))
    kk = (1.0 - lb) * jax.nn.sigmoid(-af)
    b = _mm_hi(tri, logf)
    q = _silu(a_ref[rows, 0:512])
    v = a_ref[rows, 1024:1536]
    sb = min(SUB_A, c)
    trow = lax.broadcasted_iota(jnp.int32, (sb, 1), 0)
    intra = []
    for h in range(A_HEADS):
        sl = slice(h * 128, (h + 1) * 128)
        qh, kh, vh, bh = q[:, sl], kk[:, sl], v[:, sl], b[:, sl]
        blocks = []
        for i in range(c // sb):
            rs = slice(i * sb, (i + 1) * sb)
            qi, bi = qh[rs], bh[rs]
            oi = jnp.zeros((sb, A_VAL), F32)
            if i > 0:
                r = bh[i * sb - 1:i * sb]
                att = _mm_nt(qi * jnp.exp(bi - r), kh[:i * sb] * jnp.exp(r - bh[:i * sb]))
                oi = _mm(att, vh[:i * sb])
            ki, vi = kh[rs], vh[rs]
            for s in range(sb):
                msk = trow >= s
                p = _masked_exp(msk, bi - bi[s:s + 1]) * qi * ki[s:s + 1]
                oi = oi + jnp.sum(p, axis=-1, keepdims=True) * vi[s:s + 1]
            blocks.append(oi)
        intra.append(blocks[0] if len(blocks) == 1 else jnp.concatenate(blocks, axis=0))
    bl = b[c - 1:c]
    return intra, q * jnp.exp(b), kk * jnp.exp(bl - b), v, jnp.exp(bl)


def _hgrn_post(units, c, a_ref, g_ref, y_ref, st_scr):
    heads = [slice(h * 128, (h + 1) * 128) for h in range(A_HEADS)]
    jobs = [(r0, g * A_HEADS + h, pre, sl) for r0, g, pre in units for h, sl in enumerate(heads)]
    sts = [st_scr[j] for _, j, _, _ in jobs]
    outs = [pre[0][(j % A_HEADS)] + _mm_nt(pre[1][:, sl], st) for (_, j, pre, sl), st in zip(jobs, sts)]
    for (_, j, pre, sl), st in zip(jobs, sts):
        _, _, kd, v, ebl = pre
        st_scr[j] = st * ebl[:, sl] + _mm_tn(v[:, sl], kd[:, sl])
    for (r0, _, _, sl), o in zip(jobs, outs):
        rows = pl.ds(r0, c)
        y_ref[rows, sl] = _rms(o) * g_ref[:, sl] * _silu(a_ref[rows, 1536 + sl.start:1536 + sl.stop])


def _hgrn_body(a_ref, lb_ref, g_ref, s0_ref, y_ref, so_ref, st_scr, *, c, nchunks, G):
    l = pl.program_id(1)

    @pl.when(l == 0)
    def _():
        for g in range(G):
            for h in range(A_HEADS):
                st_scr[g * A_HEADS + h] = s0_ref[g, h].T

    if G > 1:
        units = [(g * c, g, _hgrn_chunk_pre(g * c, c, a_ref, lb_ref)) for g in range(G)]
        _hgrn_post(units, c, a_ref, g_ref, y_ref, st_scr)
    else:
        pre = [_hgrn_chunk_pre(i * c, c, a_ref, lb_ref) for i in range(nchunks)]
        for i, p in enumerate(pre):
            _hgrn_post([(i * c, 0, p)], c, a_ref, g_ref, y_ref, st_scr)

    @pl.when(l == pl.num_programs(1) - 1)
    def _():
        for g in range(G):
            for h in range(A_HEADS):
                so_ref[g, h] = st_scr[g * A_HEADS + h].T


def _hgrn(proj, B, L, tl, c, G, lb, g, s0, ls):
    assert G == 1 or (L == tl == c and B % G == 0)
    nl = L // tl
    tr = G * tl
    return pl.pallas_call(
        functools.partial(_hgrn_body, c=c, nchunks=tl // c, G=G),
        grid=(B // G, nl),
        in_specs=[pl.BlockSpec((tr, 2048), lambda b, l: (b * nl + l, COL_A // 2048)),
                  pl.BlockSpec((1, 512), lambda b, l: (0, 0)),
                  pl.BlockSpec((1, 512), lambda b, l: (0, 0)),
                  pl.BlockSpec((None, G, A_HEADS, A_KEY, A_VAL), lambda b, l: (ls, b, 0, 0, 0))],
        out_specs=[pl.BlockSpec((tr, 512), lambda b, l: (b * nl + l, 0)),
                   pl.BlockSpec((G, A_HEADS, A_KEY, A_VAL), lambda b, l: (b, 0, 0, 0))],
        out_shape=[jax.ShapeDtypeStruct((B * L, 512), F32),
                   jax.ShapeDtypeStruct((B, A_HEADS, A_KEY, A_VAL), F32)],
        scratch_shapes=[pltpu.VMEM((G * A_HEADS, A_VAL, A_KEY), F32)],
        compiler_params=pltpu.CompilerParams(
            dimension_semantics=("arbitrary", "arbitrary"), vmem_limit_bytes=VMEM_LIMIT),
        name="hgrn2",
    )(proj, lb, g, s0)


def _mlstm_chunk(r0, c, G, qk_ref, v_ref, og_ref, sm_ref, bias_ref, g_ref, y_ref, c_scr, n_scr, m_scr):
    n = G * c
    R = B_HEADS * n
    rows = pl.ds(r0, n)
    rn, cn = _iotas(n)
    seq = (rn // c) == (cn // c)
    pre = sm_ref[rows, :] + bias_ref[...]
    lf = -_softplus(-pre)
    bcum = _mm_hi((seq & (rn >= cn)).astype(F32), lf)
    brev = _mm_hi((seq & (rn < cn)).astype(F32), lf)
    stack = lambda f: jnp.concatenate([f(h) for h in range(B_HEADS)], axis=0)
    q = stack(lambda h: qk_ref[rows, h * 64:(h + 1) * 64]) * (B_QK ** -0.5)
    k = stack(lambda h: qk_ref[rows, 256 + h * 64:256 + (h + 1) * 64])
    v = stack(lambda h: v_ref[rows, h * 128:(h + 1) * 128])
    ic = stack(lambda h: pre[:, h:h + 1])
    bc = stack(lambda h: bcum[:, 4 + h:5 + h])
    gg = stack(lambda h: brev[:, 4 + h:5 + h]) + ic
    m_st = m_scr[:, 0:1]

    groups = range(B_HEADS * G)
    sls = [slice(j * c, (j + 1) * c) for j in groups]
    cms = [c_scr[j] for j in groups]
    njs = [n_scr[j] for j in groups]
    qc = jnp.concatenate([_mm_nt(q[sl], cm) for sl, cm in zip(sls, cms)], axis=0)
    qn = jnp.concatenate([jnp.sum(q[sl] * nj, -1, keepdims=True) for sl, nj in zip(sls, njs)], axis=0)
    w_olds, wks = [], []
    for j in groups:
        hh, g_ = divmod(j, G)
        m = m_scr[j * c:j * c + 1, 0:1]
        bl = bcum[(g_ + 1) * c - 1:(g_ + 1) * c, 4 + hh:5 + hh]
        m_new = jnp.maximum(bl + m, jnp.max(gg[sls[j]], 0, keepdims=True))
        w_olds.append(jnp.exp(bl + m - m_new))
        wks.append(jnp.exp(gg[sls[j]] - m_new))
        m_scr[sls[j], :] = jnp.broadcast_to(m_new, (c, 128))
    for j in groups:
        c_scr[j] = w_olds[j] * cms[j] + _mm_tn(v[sls[j]] * wks[j], k[sls[j]])
        n_scr[j] = w_olds[j] * njs[j] + jnp.sum(wks[j] * k[sls[j]], 0, keepdims=True)

    ri, ci = _iotas(R)
    incl = ((ri // c) == (ci // c)) & (ri >= ci)
    eye = ri == ci
    logd = jnp.where(incl, bc - _col2row(bc, eye) + _col2row(ic, eye), NEG_BIG)
    m_inter = bc + m_st
    m_t = jnp.maximum(m_inter, jnp.max(logd, -1, keepdims=True))
    w_inter = jnp.exp(m_inter - m_t)
    sc = _mm_nt(q, k) * _masked_exp(incl, logd - m_t)
    num = w_inter * qc + _mm(sc, v)
    den = w_inter * qn + jnp.sum(sc, -1, keepdims=True)
    hh = num / jnp.maximum(jnp.abs(den), jnp.exp(-m_t))
    for h in range(B_HEADS):
        sl = slice(h * 128, (h + 1) * 128)
        y_ref[rows, sl] = _rms(hh[h * n:(h + 1) * n]) * g_ref[:, sl] * jax.nn.sigmoid(og_ref[rows, sl])


def _mlstm_body(qk_ref, v_ref, og_ref, sm_ref, bias_ref, g_ref, c0_ref, n0_ref, m0_ref,
                y_ref, co_ref, no_ref, mo_ref, c_scr, n_scr, m_scr, *, tl, c, G):
    l = pl.program_id(1)

    @pl.when(l == 0)
    def _():
        for h in range(B_HEADS):
            for g in range(G):
                j = h * G + g
                c_scr[j] = c0_ref[g, h]
                n_scr[j] = n0_ref[g, h:h + 1, :]
                m_scr[j * c:(j + 1) * c, :] = jnp.broadcast_to(m0_ref[g, h:h + 1, :], (c, 128))

    args = (qk_ref, v_ref, og_ref, sm_ref, bias_ref, g_ref, y_ref, c_scr, n_scr, m_scr)
    if tl == c:
        _mlstm_chunk(0, c, G, *args)
    else:
        def step(i, carry):
            _mlstm_chunk(pl.multiple_of(i * c, c), c, G, *args)
            return carry
        lax.fori_loop(0, tl // c, step, 0)

    @pl.when(l == pl.num_programs(1) - 1)
    def _():
        for h in range(B_HEADS):
            for g in range(G):
                j = h * G + g
                co_ref[g, h] = c_scr[j]
                no_ref[g, h:h + 1, :] = n_scr[j]
                mo_ref[g, h:h + 1, :] = m_scr[j * c:j * c + 1, 0:1]


def _mlstm(proj, B, L, tl, c, G, bias_row, g, c0, n0, m0, ls):
    assert G == 1 or (L == tl == c and B % G == 0)
    nl = L // tl
    tr = G * tl
    cb = COL_B // 512
    row = lambda b, l: b * nl + l
    return pl.pallas_call(
        functools.partial(_mlstm_body, tl=tl, c=c, G=G),
        grid=(B // G, nl),
        in_specs=[pl.BlockSpec((tr, 512), lambda b, l: (row(b, l), cb)),
                  pl.BlockSpec((tr, 512), lambda b, l: (row(b, l), cb + 1)),
                  pl.BlockSpec((tr, 512), lambda b, l: (row(b, l), cb + 2)),
                  pl.BlockSpec((tr, 128), lambda b, l: (row(b, l), COL_S // 128)),
                  pl.BlockSpec((1, 128), lambda b, l: (0, 0)),
                  pl.BlockSpec((1, 512), lambda b, l: (0, 0)),
                  pl.BlockSpec((None, G, B_HEADS, B_VAL, B_QK), lambda b, l: (ls, b, 0, 0, 0)),
                  pl.BlockSpec((None, G, B_HEADS, B_QK), lambda b, l: (ls, b, 0, 0)),
                  pl.BlockSpec((None, G, B_HEADS, 1), lambda b, l: (ls, b, 0, 0))],
        out_specs=[pl.BlockSpec((tr, 512), lambda b, l: (row(b, l), 0)),
                   pl.BlockSpec((G, B_HEADS, B_VAL, B_QK), lambda b, l: (b, 0, 0, 0)),
                   pl.BlockSpec((G, B_HEADS, B_QK), lambda b, l: (b, 0, 0)),
                   pl.BlockSpec((G, B_HEADS, 1), lambda b, l: (b, 0, 0))],
        out_shape=[jax.ShapeDtypeStruct((B * L, 512), F32),
                   jax.ShapeDtypeStruct((B, B_HEADS, B_VAL, B_QK), F32),
                   jax.ShapeDtypeStruct((B, B_HEADS, B_QK), F32),
                   jax.ShapeDtypeStruct((B, B_HEADS, 1), F32)],
        scratch_shapes=[pltpu.VMEM((B_HEADS * G, B_VAL, B_QK), F32),
                        pltpu.VMEM((B_HEADS * G, 1, B_QK), F32),
                        pltpu.VMEM((B_HEADS * G * c, 128), F32)],
        compiler_params=pltpu.CompilerParams(
            dimension_semantics=("arbitrary", "arbitrary"), vmem_limit_bytes=VMEM_LIMIT),
        name="mlstm",
    )(proj, proj, proj, proj, bias_row, g, c0, n0, m0)


def _unit_lower_inverse_m1(mats, c, ri, ci):
    if c <= SUB:
        ads, aoffs = mats, None
    else:
        same = (ri // SUB) == (ci // SUB)
        ads = [jnp.where(same, a, 0.0) for a in mats]
        aoffs = [a - ad for a, ad in zip(mats, ads)]
    ts = [-ad for ad in ads]
    ps = ads
    n = 2
    while n < min(c, SUB):
        ps = [_mm(p, p) for p in ps]
        ts = [t + p + _mm(t, p) for t, p in zip(ts, ps)]
        n *= 2
    if aoffs is None:
        return ts
    assert c // SUB == 4
    nns = [aoff + _mm(t, aoff) for t, aoff in zip(ts, aoffs)]
    n2s = [_mm(nn, nn) for nn in nns]
    xs = [t + n2 + _mm(n2, t) for t, n2 in zip(ts, n2s)]
    return [x - nn - _mm(nn, x) for x, nn in zip(xs, nns)]


def _gdn_chunks(r0s, c, G, cv_scr, x_ref, sm_ref, arow_ref, dtrow_ref, g_ref, y_ref, s_scr):
    pre = [_gdn_chunk_pre(r0, c, G, cv_scr, sm_ref, arow_ref, dtrow_ref) for r0 in r0s]
    ri, ci = _iotas(C_HEADS * G * c)
    invs = _unit_lower_inverse_m1([p[0] for p in pre], c, ri, ci)
    for r0, p, u in zip(r0s, pre, invs):
        _gdn_chunk_post(r0, c, G, p, u, x_ref, g_ref, y_ref, s_scr)


def _gdn_chunk_pre(r0, c, G, cv_scr, sm_ref, arow_ref, dtrow_ref):
    n = G * c
    R = C_HEADS * n
    rows = pl.ds(r0, n)
    rn, cn = _iotas(n)
    seq = (rn // c) == (cn // c)
    sm = sm_ref[rows, :]
    beta_all = jax.nn.sigmoid(sm)
    g_all = arow_ref[...] * _softplus(sm + dtrow_ref[...])
    bcum = _mm_hi((seq & (rn >= cn)).astype(F32), g_all)
    brev = _mm_hi((seq & (rn < cn)).astype(F32), g_all)
    stack = lambda f: jnp.concatenate([f(h) for h in range(C_HEADS)], axis=0)
    q = stack(lambda h: cv_scr[rows, h * 128:(h + 1) * 128])
    k = stack(lambda h: cv_scr[rows, 512 + h * 128:512 + (h + 1) * 128])
    v = stack(lambda h: cv_scr[rows, 1024 + h * 128:1024 + (h + 1) * 128])
    beta = stack(lambda h: beta_all[:, 8 + h:9 + h])
    bc = stack(lambda h: bcum[:, 12 + h:13 + h])
    kdec = jnp.exp(stack(lambda h: brev[:, 12 + h:13 + h]))
    q = q * lax.rsqrt(jnp.sum(q * q, -1, keepdims=True) + NORM_EPS) * (C_KEY ** -0.5)
    k = k * lax.rsqrt(jnp.sum(k * k, -1, keepdims=True) + NORM_EPS)

    ri, ci = _iotas(R)
    grp = (ri // c) == (ci // c)
    incl = grp & (ri >= ci)
    strict = grp & (ri > ci)
    diff = bc - _col2row(bc, ri == ci)
    dec = jnp.exp(jnp.where(incl, diff, 0.0))
    a = beta * _mm_nt(k, k) * jnp.where(strict, dec, 0.0)
    ebc = jnp.exp(bc)
    rhs = jnp.concatenate([v * beta, k * (beta * ebc)], axis=-1)
    att = _mm_nt(q, k) * jnp.where(incl, dec, 0.0)
    return a, rhs, att, q * ebc, k * kdec, bcum


def _gdn_chunk_post(r0, c, G, pre, inv_m1, x_ref, g_ref, y_ref, s_scr):
    _, rhs, att, qe, kd, bcum = pre
    n = G * c
    rows = pl.ds(r0, n)
    sol = rhs + _mm(inv_m1, rhs)
    groups = range(C_HEADS * G)
    sls = [slice(j * c, (j + 1) * c) for j in groups]
    s_old = [s_scr[j] for j in groups]
    v_new = [sol[sl, :C_VAL] - _mm(sol[sl, C_VAL:], s) for sl, s in zip(sls, s_old)]
    o_inter = [_mm(qe[sl], s) for sl, s in zip(sls, s_old)]
    for j in groups:
        hh, gg = divmod(j, G)
        bl = bcum[(gg + 1) * c - 1:(gg + 1) * c, 12 + hh:13 + hh]
        s_scr[j] = jnp.exp(bl) * s_old[j] + _mm_tn(kd[sls[j]], v_new[j])
    o = jnp.concatenate(o_inter, axis=0) + _mm(att, jnp.concatenate(v_new, axis=0))
    cg = x_ref[rows, CONV_CH:2048]
    for h in range(C_HEADS):
        sl = slice(h * 128, (h + 1) * 128)
        y_ref[rows, sl] = _rms(o[h * n:(h + 1) * n]) * g_ref[:, sl] * _silu(cg[:, sl])


def _gdn_body(x_ref, sm_ref, cw_ref, arow_ref, dtrow_ref, g_ref, s0_ref, cs_ref,
              y_ref, so_ref, cso_ref, s_scr, xb_scr, cv_scr, *, tl, c, G):
    l = pl.program_id(1)
    args = (cv_scr, x_ref, sm_ref, arow_ref, dtrow_ref, g_ref, y_ref, s_scr)

    @pl.when(l == 0)
    def _():
        for h in range(C_HEADS):
            for g in range(G):
                s_scr[h * G + g] = s0_ref[g, h]

    if G > 1:
        for g in range(G):
            xb_scr[0:8, :] = cs_ref[g]
            xb_scr[8:8 + c, :] = x_ref[g * c:(g + 1) * c, 0:CONV_CH]
            conv = xb_scr[5:5 + c, :] * cw_ref[0:1, :]
            for j in range(1, CONV_W):
                conv = conv + xb_scr[5 + j:5 + j + c, :] * cw_ref[j:j + 1, :]
            cv_scr[g * c:(g + 1) * c, :] = _silu(conv)
            cso_ref[g] = xb_scr[c + 5:c + 8, :]
        _gdn_chunks([0], c, G, *args)
    else:
        @pl.when(l == 0)
        def _():
            xb_scr[0:8, :] = cs_ref[0]

        xb_scr[8:8 + tl, :] = x_ref[:, 0:CONV_CH]
        conv = xb_scr[5:5 + tl, :] * cw_ref[0:1, :]
        for j in range(1, CONV_W):
            conv = conv + xb_scr[5 + j:5 + j + tl, :] * cw_ref[j:j + 1, :]
        cv_scr[...] = _silu(conv)

        _gdn_chunks([i * c for i in range(tl // c)], c, 1, *args)

        @pl.when(l == pl.num_programs(1) - 1)
        def _():
            cso_ref[0] = xb_scr[tl + 5:tl + 8, :]

        xb_scr[0:8, :] = xb_scr[tl:tl + 8, :]

    @pl.when(l == pl.num_programs(1) - 1)
    def _():
        for h in range(C_HEADS):
            for g in range(G):
                so_ref[g, h] = s_scr[h * G + g]


def _gdn(proj, B, L, tl, c, G, conv_w, arow, dtrow, g, s0, cs_pad, ls):
    assert G == 1 or (L == tl == c and B % G == 0)
    nl = L // tl
    tr = G * tl
    row = lambda b, l: b * nl + l
    return pl.pallas_call(
        functools.partial(_gdn_body, tl=tl, c=c, G=G),
        grid=(B // G, nl),
        in_specs=[pl.BlockSpec((tr, 2048), lambda b, l: (row(b, l), COL_C // 2048)),
                  pl.BlockSpec((tr, 128), lambda b, l: (row(b, l), COL_S // 128)),
                  pl.BlockSpec((CONV_W, CONV_CH), lambda b, l: (0, 0)),
                  pl.BlockSpec((1, 128), lambda b, l: (0, 0)),
                  pl.BlockSpec((1, 128), lambda b, l: (0, 0)),
                  pl.BlockSpec((1, 512), lambda b, l: (0, 0)),
                  pl.BlockSpec((None, G, C_HEADS, C_KEY, C_VAL), lambda b, l: (ls, b, 0, 0, 0)),
                  pl.BlockSpec((None, G, 8, CONV_CH), lambda b, l: (ls, b, 0, 0))],
        out_specs=[pl.BlockSpec((tr, 512), lambda b, l: (row(b, l), 0)),
                   pl.BlockSpec((G, C_HEADS, C_KEY, C_VAL), lambda b, l: (b, 0, 0, 0)),
                   pl.BlockSpec((G, CONV_W - 1, CONV_CH), lambda b, l: (b, 0, 0))],
        out_shape=[jax.ShapeDtypeStruct((B * L, 512), F32),
                   jax.ShapeDtypeStruct((B, C_HEADS, C_KEY, C_VAL), F32),
                   jax.ShapeDtypeStruct((B, CONV_W - 1, CONV_CH), F32)],
        scratch_shapes=[pltpu.VMEM((C_HEADS * G, C_KEY, C_VAL), F32),
                        pltpu.VMEM((tl + 8, CONV_CH), F32),
                        pltpu.VMEM((tr, CONV_CH), F32)],
        compiler_params=pltpu.CompilerParams(
            dimension_semantics=("arbitrary", "arbitrary"), vmem_limit_bytes=VMEM_LIMIT),
        name="gdn",
    )(proj, proj, conv_w, arow, dtrow, g, s0, cs_pad)


def _pad_row(vals, start):
    return jnp.zeros((1, 128), F32).at[0, start:start + vals.shape[0]].set(vals.astype(F32))


def _layer(h, B, L, tl, c, G, st, ls, w, lw, l):
    st_a, st_c, st_n, st_m, st_g, st_conv = st
    proj = _proj(h, w["w_in"], l)
    ya, na = _hgrn(proj, B, L, tl, c, G, lw["lb"], lw["a_g"], st_a, ls)
    yb, nc, nn, nm = _mlstm(proj, B, L, tl, c, G, lw["b_bias"], lw["b_g"], st_c, st_n, st_m, ls)
    yc, ng, nv = _gdn(proj, B, L, tl, c, G, lw["conv_w"], lw["c_arow"], lw["c_dtrow"], lw["c_g"], st_g, st_conv, ls)
    h = _post(ya, yb, yc, proj, h, w["wb"], w["wo"], w["ln1_g"], w["ln1_b"], w["wfi"], w["wfo"],
              w["ln2_g"], w["ln2_b"], l)
    return h, (na, nc, nn, nm[..., 0], ng, nv)


def kernel(x_prompt, x_sample, state_hgrn, state_mlstm_c, state_mlstm_n, state_mlstm_m, state_gdn,
           state_gdn_conv, w_in, lb_logits, a_norm_g, b_mi, b_mf, b_norm_g, conv_w, a_log, dt_bias,
           c_norm_g, w_branch, w_out, ln1_g, ln1_b, w_ffn_in, w_ffn_out, ln2_g, ln2_b):
    Bp, Lp, _ = x_prompt.shape
    Bs, Ls, _ = x_sample.shape
    cp, cs = math.gcd(Lp, CHUNK), math.gcd(Ls, CHUNK)
    tlp = min(Lp, 256)

    w = dict(w_in=_reorder_w_in(w_in), wb=w_branch.astype(BF16), wo=w_out.astype(BF16),
             wfi=w_ffn_in.astype(BF16), wfo=w_ffn_out.astype(BF16),
             ln1_g=ln1_g[:, None].astype(F32), ln1_b=ln1_b[:, None].astype(F32),
             ln2_g=ln2_g[:, None].astype(F32), ln2_b=ln2_b[:, None].astype(F32))
    lb_w = jax.nn.softmax(lb_logits.astype(F32), axis=0)
    lb_all = jnp.cumsum(lb_w, axis=0) - lb_w[0]
    hist_pad = ((0, 0), (0, 0), (8 - (CONV_W - 1), 0), (0, 0))

    h_p = x_prompt.reshape(Bp * Lp, D_MODEL)
    h_s = x_sample.reshape(Bs * Ls, D_MODEL)
    st_p = (jnp.zeros((1, Bp, A_HEADS, A_KEY, A_VAL), F32), jnp.zeros((1, Bp, B_HEADS, B_VAL, B_QK), F32),
            jnp.zeros((1, Bp, B_HEADS, B_QK), F32), jnp.zeros((1, Bp, B_HEADS, 1), F32),
            jnp.zeros((1, Bp, C_HEADS, C_KEY, C_VAL), F32), jnp.zeros((1, Bp, 8, CONV_CH), F32))
    st_s = (state_hgrn.astype(F32), state_mlstm_c.astype(F32), state_mlstm_n.astype(F32),
            state_mlstm_m.astype(F32)[..., None], state_gdn.astype(F32),
            jnp.pad(state_gdn_conv.astype(F32), hist_pad))
    new_p, new_s = [], []
    for l in range(DEPTH):
        lw = dict(lb=lb_all[l][None], a_g=a_norm_g[l][None].astype(F32),
                  b_bias=_pad_row(jnp.concatenate([b_mi[l], b_mf[l]]), 0), b_g=b_norm_g[l][None].astype(F32),
                  conv_w=conv_w[l].astype(F32), c_arow=_pad_row(-jnp.exp(a_log[l].astype(F32)), 12),
                  c_dtrow=_pad_row(dt_bias[l], 12), c_g=c_norm_g[l][None].astype(F32))
        h_p, ns_p = _layer(h_p, Bp, Lp, tlp, cp, 1, st_p, 0, w, lw, l)
        h_s, ns_s = _layer(h_s, Bs, Ls, Ls, cs, SAMPLE_G, st_s, l, w, lw, l)
        new_p.append(ns_p)
        new_s.append(ns_s)

    outs = [h_p.reshape(Bp, Lp, D_MODEL), h_s.reshape(Bs, Ls, D_MODEL)]
    for i, ref in enumerate((state_hgrn, state_mlstm_c, state_mlstm_n, state_mlstm_m, state_gdn,
                             state_gdn_conv)):
        outs.append(jnp.stack([ns[i] for ns in new_p]).astype(ref.dtype))
        outs.append(jnp.stack([ns[i] for ns in new_s]).astype(ref.dtype))
    return tuple(outs)
```

```python
import functools
import math

import jax
import jax.numpy as jnp
from jax import lax
from jax.experimental import pallas as pl
from jax.experimental.pallas import tpu as pltpu

F32 = jnp.float32
BF16 = jnp.bfloat16
MXU_DT = BF16

D_MODEL = 1024
DEPTH = 4
A_HEADS, A_KEY, A_VAL = 4, 128, 128
B_HEADS, B_QK, B_VAL = 4, 64, 128
C_HEADS, C_KEY, C_VAL = 4, 128, 128
CONV_W = 4
CONV_CH = C_HEADS * (2 * C_KEY + C_VAL)
N_BRANCH = 3
BRANCH_W = 512
FFN_HIDDEN = 2816
CHUNK = 64
ALPHA = (2 * DEPTH) ** 0.25
LN_EPS = 1e-5
NORM_EPS = 1e-6
NEG_BIG = -1e30
LOG2E = 1.0 / math.log(2.0)

COL_A = 0
COL_C = 2048
COL_MG = 4096
COL_B = 7168
COL_S = 8704
N_PROJ = 8832
PROJ_TN = N_PROJ // 3

VMEM_LIMIT = 56 * 1024 * 1024


def _reorder_w_in(w_in):
    a = w_in[..., 0:2048]
    b = w_in[..., 2048:3584]
    s1 = w_in[..., 3584:3592]
    c = w_in[..., 3592:5640]
    s2 = w_in[..., 5640:5648]
    mg = w_in[..., 5648:8720]
    pad = jnp.zeros(w_in.shape[:-1] + (N_PROJ - COL_S - 16,), w_in.dtype)
    return jnp.concatenate([a, c, mg, b, s1, s2, pad], axis=-1).astype(BF16)


def _proj_body(x_ref, w_ref, o_ref):
    o_ref[...] = jnp.dot(x_ref[...].astype(BF16), w_ref[...], preferred_element_type=F32)


def _proj(h, w, l):
    T = h.shape[0]
    tm = min(1024, T)
    return pl.pallas_call(
        _proj_body,
        grid=(T // tm, N_PROJ // PROJ_TN),
        in_specs=[pl.BlockSpec((tm, D_MODEL), lambda i, j: (i, 0)),
                  pl.BlockSpec((None, D_MODEL, PROJ_TN), lambda i, j: (l, 0, j))],
        out_specs=pl.BlockSpec((tm, PROJ_TN), lambda i, j: (i, j)),
        out_shape=jax.ShapeDtypeStruct((T, N_PROJ), F32),
        compiler_params=pltpu.CompilerParams(
            dimension_semantics=("arbitrary", "arbitrary"), vmem_limit_bytes=VMEM_LIMIT),
        name="in_proj",
    )(h, w)


def _ln(x, g, b):
    mu = jnp.mean(x, -1, keepdims=True)
    xc = x - mu
    var = jnp.mean(xc * xc, -1, keepdims=True)
    return xc * lax.rsqrt(var + LN_EPS) * g + b


FFN_HC = 256


def _post_body(ya_ref, yb_ref, yc_ref, mg0_ref, mg1_ref, mg2_ref, h_ref, wb_ref, wo_ref, g1_ref, b1_ref,
               wfi_ref, wfo_ref, g2_ref, b2_ref, o_ref, a_scr):
    mgs = (mg0_ref, mg1_ref, mg2_ref)
    ys = (ya_ref, yb_ref, yc_ref)
    merged = None
    for n in range(N_BRANCH):
        y = ys[n][...].astype(BF16)
        z = jnp.dot(y, wb_ref[n], preferred_element_type=F32)
        t = jax.nn.sigmoid(mgs[n][...]) * z
        merged = t if merged is None else merged + t
    mix = jnp.dot(merged.astype(BF16), wo_ref[...], preferred_element_type=F32)
    h1 = _ln(ALPHA * h_ref[...] + mix, g1_ref[...], b1_ref[...])
    h1b = h1.astype(BF16)
    for j in range(FFN_HIDDEN // FFN_HC):
        gate = jnp.dot(h1b, wfi_ref[:, j * FFN_HC:(j + 1) * FFN_HC], preferred_element_type=F32)
        up = jnp.dot(h1b, wfi_ref[:, FFN_HIDDEN + j * FFN_HC:FFN_HIDDEN + (j + 1) * FFN_HC],
                     preferred_element_type=F32)
        a_scr[:, j * FFN_HC:(j + 1) * FFN_HC] = (gate * jax.nn.sigmoid(gate) * up).astype(BF16)
    ff = jnp.dot(a_scr[...], wfo_ref[...], preferred_element_type=F32)
    o_ref[...] = _ln(ALPHA * h1 + ff, g2_ref[...], b2_ref[...])


def _layer_spec(shape, l):
    nd = len(shape)
    return pl.BlockSpec((None,) + shape, lambda i: (l,) + (0,) * nd, pipeline_mode=pl.Buffered(1))


def _post(ya, yb, yc, proj, h, wb, wo, g1, b1, wfi, wfo, g2, b2, l):
    T = h.shape[0]
    tm = min(256, T)
    mg_blk = COL_MG // D_MODEL
    return pl.pallas_call(
        _post_body,
        grid=(T // tm,),
        in_specs=[pl.BlockSpec((tm, BRANCH_W), lambda i: (i, 0)),
                  pl.BlockSpec((tm, BRANCH_W), lambda i: (i, 0)),
                  pl.BlockSpec((tm, BRANCH_W), lambda i: (i, 0)),
                  pl.BlockSpec((tm, D_MODEL), lambda i: (i, mg_blk)),
                  pl.BlockSpec((tm, D_MODEL), lambda i: (i, mg_blk + 1)),
                  pl.BlockSpec((tm, D_MODEL), lambda i: (i, mg_blk + 2)),
                  pl.BlockSpec((tm, D_MODEL), lambda i: (i, 0)),
                  _layer_spec((N_BRANCH, BRANCH_W, D_MODEL), l),
                  _layer_spec((D_MODEL, D_MODEL), l),
                  _layer_spec((1, D_MODEL), l), _layer_spec((1, D_MODEL), l),
                  _layer_spec((D_MODEL, 2 * FFN_HIDDEN), l),
                  _layer_spec((FFN_HIDDEN, D_MODEL), l),
                  _layer_spec((1, D_MODEL), l), _layer_spec((1, D_MODEL), l)],
        out_specs=pl.BlockSpec((tm, D_MODEL), lambda i: (i, 0)),
        out_shape=jax.ShapeDtypeStruct((T, D_MODEL), F32),
        scratch_shapes=[pltpu.VMEM((tm, FFN_HIDDEN), BF16)],
        compiler_params=pltpu.CompilerParams(
            dimension_semantics=("arbitrary",), vmem_limit_bytes=VMEM_LIMIT),
        name="merge_ffn",
    )(ya, yb, yc, proj, proj, proj, h, wb, wo, g1, b1, wfi, wfo, g2, b2)


SAMPLE_G = 8
SUB = 16
SUB_A = 16
_HI = lax.Precision.HIGHEST


def _mm(a, b):
    return jnp.dot(a.astype(MXU_DT), b.astype(MXU_DT), preferred_element_type=F32)


def _mm_nt(a, b):
    return lax.dot_general(a.astype(MXU_DT), b.astype(MXU_DT), (((1,), (1,)), ((), ())),
                           preferred_element_type=F32)


def _mm_tn(a, b):
    return lax.dot_general(a.astype(MXU_DT), b.astype(MXU_DT), (((0,), (0,)), ((), ())),
                           preferred_element_type=F32)


def _mm_hi(a, b):
    return jnp.dot(a, b, precision=_HI, preferred_element_type=F32)


def _silu(x):
    return x * jax.nn.sigmoid(x)


def _softplus(x):
    return jnp.maximum(x, 0.0) + jnp.log1p(jnp.exp(-jnp.abs(x)))


def _masked_exp(mask, t):
    return jnp.where(mask, jnp.exp(jnp.where(mask, t, 0.0)), 0.0)


def _rms(o):
    return o * lax.rsqrt(jnp.mean(o * o, -1, keepdims=True) + NORM_EPS)


def _col2row(col, eye):
    return jnp.sum(jnp.where(eye, col, 0.0), axis=0, keepdims=True)


def _iotas(c):
    ri = lax.broadcasted_iota(jnp.int32, (c, c), 0)
    ci = lax.broadcasted_iota(jnp.int32, (c, c), 1)
    return ri, ci


def _hgrn_chunk_pre(r0, c, a_ref, lb_ref):
    rows = pl.ds(r0, c)
    ri, ci = _iotas(c)
    tri = (ri >= ci).astype(F32)
    lb = lb_ref[...]
    af = a_ref[rows, 512:1024]
    logf = jnp.log(lb + (1.0 - lb) * jax.nn.sigmoid(af))
    kk = (1.0 - lb) * jax.nn.sigmoid(-af)
    b = _mm_hi(tri, logf)
    b2 = b * LOG2E
    q = _silu(a_ref[rows, 0:512])
    v = a_ref[rows, 1024:1536]
    sb = min(SUB_A, c)
    trow = lax.broadcasted_iota(jnp.int32, (sb, 1), 0)
    intra = []
    for h in range(A_HEADS):
        sl = slice(h * 128, (h + 1) * 128)
        qh, kh, vh, bh = q[:, sl], kk[:, sl], v[:, sl], b[:, sl]
        blocks = []
        for i in range(c // sb):
            rs = slice(i * sb, (i + 1) * sb)
            qi, bi = qh[rs], bh[rs]
            oi = jnp.zeros((sb, A_VAL), F32)
            if i > 0:
                r = bh[i * sb - 1:i * sb]
                att = _mm_nt(qi * jnp.exp(bi - r), kh[:i * sb] * jnp.exp(r - bh[:i * sb]))
                oi = _mm(att, vh[:i * sb])
            ki, vi, b2i = kh[rs], vh[rs], b2[rs, sl]
            for s in range(sb):
                e = jnp.where(trow >= s, jnp.exp2(b2i - b2i[s:s + 1]), 0.0)
                oi = oi + jnp.sum(e * qi * ki[s:s + 1], axis=-1, keepdims=True) * vi[s:s + 1]
            blocks.append(oi)
        intra.append(blocks[0] if len(blocks) == 1 else jnp.concatenate(blocks, axis=0))
    bl = b[c - 1:c]
    return intra, q * jnp.exp(b), kk * jnp.exp(bl - b), v, jnp.exp(bl)


def _hgrn_post(units, c, a_ref, g_ref, y_ref, st_scr):
    heads = [slice(h * 128, (h + 1) * 128) for h in range(A_HEADS)]
    jobs = [(r0, g * A_HEADS + h, pre, sl) for r0, g, pre in units for h, sl in enumerate(heads)]
    sts = [st_scr[j] for _, j, _, _ in jobs]
    outs = [pre[0][(j % A_HEADS)] + _mm_nt(pre[1][:, sl], st) for (_, j, pre, sl), st in zip(jobs, sts)]
    for (_, j, pre, sl), st in zip(jobs, sts):
        _, _, kd, v, ebl = pre
        st_scr[j] = st * ebl[:, sl] + _mm_tn(v[:, sl], kd[:, sl])
    for (r0, _, _, sl), o in zip(jobs, outs):
        rows = pl.ds(r0, c)
        y_ref[rows, sl] = _rms(o) * g_ref[:, sl] * _silu(a_ref[rows, 1536 + sl.start:1536 + sl.stop])


def _hgrn_body(a_ref, lb_ref, g_ref, s0_ref, y_ref, so_ref, st_scr, *, c, nchunks, G):
    l = pl.program_id(1)

    @pl.when(l == 0)
    def _():
        for g in range(G):
            for h in range(A_HEADS):
                st_scr[g * A_HEADS + h] = s0_ref[g, h].T

    if G > 1:
        units = [(g * c, g, _hgrn_chunk_pre(g * c, c, a_ref, lb_ref)) for g in range(G)]
        _hgrn_post(units, c, a_ref, g_ref, y_ref, st_scr)
    else:
        pre = _hgrn_chunk_pre(0, c, a_ref, lb_ref)
        for i in range(nchunks):
            nxt = _hgrn_chunk_pre((i + 1) * c, c, a_ref, lb_ref) if i + 1 < nchunks else None
            _hgrn_post([(i * c, 0, pre)], c, a_ref, g_ref, y_ref, st_scr)
            pre = nxt

    @pl.when(l == pl.num_programs(1) - 1)
    def _():
        for g in range(G):
            for h in range(A_HEADS):
                so_ref[g, h] = st_scr[g * A_HEADS + h].T


def _hgrn(proj, B, L, tl, c, G, lb, g, s0, ls):
    assert G == 1 or (L == tl == c and B % G == 0)
    nl = L // tl
    tr = G * tl
    return pl.pallas_call(
        functools.partial(_hgrn_body, c=c, nchunks=tl // c, G=G),
        grid=(B // G, nl),
        in_specs=[pl.BlockSpec((tr, 2048), lambda b, l: (b * nl + l, COL_A // 2048)),
                  pl.BlockSpec((1, 512), lambda b, l: (0, 0)),
                  pl.BlockSpec((1, 512), lambda b, l: (0, 0)),
                  pl.BlockSpec((None, G, A_HEADS, A_KEY, A_VAL), lambda b, l: (ls, b, 0, 0, 0))],
        out_specs=[pl.BlockSpec((tr, 512), lambda b, l: (b * nl + l, 0)),
                   pl.BlockSpec((G, A_HEADS, A_KEY, A_VAL), lambda b, l: (b, 0, 0, 0))],
        out_shape=[jax.ShapeDtypeStruct((B * L, 512), F32),
                   jax.ShapeDtypeStruct((B, A_HEADS, A_KEY, A_VAL), F32)],
        scratch_shapes=[pltpu.VMEM((G * A_HEADS, A_VAL, A_KEY), F32)],
        compiler_params=pltpu.CompilerParams(
            dimension_semantics=("arbitrary", "arbitrary"), vmem_limit_bytes=VMEM_LIMIT),
        name="hgrn2",
    )(proj, lb, g, s0)


def _mlstm_chunk(r0, c, G, qk_ref, v_ref, og_ref, sm_ref, bias_ref, g_ref, y_ref, c_scr, n_scr, m_scr):
    n = G * c
    R = B_HEADS * n
    rows = pl.ds(r0, n)
    rn, cn = _iotas(n)
    seq = (rn // c) == (cn // c)
    pre = sm_ref[rows, :] + bias_ref[...]
    lf = -_softplus(-pre)
    bcum = _mm_hi((seq & (rn >= cn)).astype(F32), lf)
    brev = _mm_hi((seq & (rn < cn)).astype(F32), lf)
    stack = lambda f: jnp.concatenate([f(h) for h in range(B_HEADS)], axis=0)
    q = stack(lambda h: qk_ref[rows, h * 64:(h + 1) * 64]) * (B_QK ** -0.5)
    k = stack(lambda h: qk_ref[rows, 256 + h * 64:256 + (h + 1) * 64])
    v = stack(lambda h: v_ref[rows, h * 128:(h + 1) * 128])
    ic = stack(lambda h: pre[:, h:h + 1])
    bc = stack(lambda h: bcum[:, 4 + h:5 + h])
    gg = stack(lambda h: brev[:, 4 + h:5 + h]) + ic
    m_st = m_scr[:, 0:1]

    groups = range(B_HEADS * G)
    sls = [slice(j * c, (j + 1) * c) for j in groups]
    cms = [c_scr[j] for j in groups]
    njs = [n_scr[j] for j in groups]
    qc = jnp.concatenate([_mm(q[sl], cm) for sl, cm in zip(sls, cms)], axis=0)
    qn = jnp.concatenate([jnp.sum(q[sl] * nj, -1, keepdims=True) for sl, nj in zip(sls, njs)], axis=0)
    w_olds, wks = [], []
    for j in groups:
        hh, g_ = divmod(j, G)
        m = m_scr[j * c:j * c + 1, 0:1]
        bl = bcum[(g_ + 1) * c - 1:(g_ + 1) * c, 4 + hh:5 + hh]
        m_new = jnp.maximum(bl + m, jnp.max(gg[sls[j]], 0, keepdims=True))
        w_olds.append(jnp.exp(bl + m - m_new))
        wks.append(jnp.exp(gg[sls[j]] - m_new))
        m_scr[sls[j], :] = jnp.broadcast_to(m_new, (c, 128))
    for j in groups:
        c_scr[j] = w_olds[j] * cms[j] + _mm_tn(k[sls[j]], v[sls[j]] * wks[j])
        n_scr[j] = w_olds[j] * njs[j] + jnp.sum(wks[j] * k[sls[j]], 0, keepdims=True)

    ri, ci = _iotas(R)
    incl = ((ri // c) == (ci // c)) & (ri >= ci)
    eye = ri == ci
    logd = jnp.where(incl, bc - _col2row(bc, eye) + _col2row(ic, eye), NEG_BIG)
    m_inter = bc + m_st
    m_t = jnp.maximum(m_inter, jnp.max(logd, -1, keepdims=True))
    w_inter = jnp.exp(m_inter - m_t)
    sc = _mm_nt(q, k) * _masked_exp(incl, logd - m_t)
    num = w_inter * qc + _mm(sc, v)
    den = w_inter * qn + jnp.sum(sc, -1, keepdims=True)
    hh = num / jnp.maximum(jnp.abs(den), jnp.exp(-m_t))
    for h in range(B_HEADS):
        sl = slice(h * 128, (h + 1) * 128)
        y_ref[rows, sl] = _rms(hh[h * n:(h + 1) * n]) * g_ref[:, sl] * jax.nn.sigmoid(og_ref[rows, sl])


def _mlstm_body(qk_ref, v_ref, og_ref, sm_ref, bias_ref, g_ref, c0_ref, n0_ref, m0_ref,
                y_ref, co_ref, no_ref, mo_ref, c_scr, n_scr, m_scr, *, tl, c, G):
    l = pl.program_id(1)

    @pl.when(l == 0)
    def _():
        for h in range(B_HEADS):
            for g in range(G):
                j = h * G + g
                c_scr[j] = c0_ref[g, h]
                n_scr[j] = n0_ref[g, h:h + 1, :]
                m_scr[j * c:(j + 1) * c, :] = jnp.broadcast_to(m0_ref[g, h:h + 1, :], (c, 128))

    args = (qk_ref, v_ref, og_ref, sm_ref, bias_ref, g_ref, y_ref, c_scr, n_scr, m_scr)
    if tl == c:
        _mlstm_chunk(0, c, G, *args)
    else:
        def step(i, carry):
            _mlstm_chunk(pl.multiple_of(i * c, c), c, G, *args)
            return carry
        lax.fori_loop(0, tl // c, step, 0)

    @pl.when(l == pl.num_programs(1) - 1)
    def _():
        for h in range(B_HEADS):
            for g in range(G):
                j = h * G + g
                co_ref[g, h] = c_scr[j]
                no_ref[g, h:h + 1, :] = n_scr[j]
                mo_ref[g, h:h + 1, :] = m_scr[j * c:j * c + 1, 0:1]


def _mlstm(proj, B, L, tl, c, G, bias_row, g, c0, n0, m0, ls):
    assert G == 1 or (L == tl == c and B % G == 0)
    nl = L // tl
    tr = G * tl
    cb = COL_B // 512
    row = lambda b, l: b * nl + l
    return pl.pallas_call(
        functools.partial(_mlstm_body, tl=tl, c=c, G=G),
        grid=(B // G, nl),
        in_specs=[pl.BlockSpec((tr, 512), lambda b, l: (row(b, l), cb)),
                  pl.BlockSpec((tr, 512), lambda b, l: (row(b, l), cb + 1)),
                  pl.BlockSpec((tr, 512), lambda b, l: (row(b, l), cb + 2)),
                  pl.BlockSpec((tr, 128), lambda b, l: (row(b, l), COL_S // 128)),
                  pl.BlockSpec((1, 128), lambda b, l: (0, 0)),
                  pl.BlockSpec((1, 512), lambda b, l: (0, 0)),
                  pl.BlockSpec((None, G, B_HEADS, B_QK, B_VAL), lambda b, l: (ls, b, 0, 0, 0)),
                  pl.BlockSpec((None, G, B_HEADS, B_QK), lambda b, l: (ls, b, 0, 0)),
                  pl.BlockSpec((None, G, B_HEADS, 1), lambda b, l: (ls, b, 0, 0))],
        out_specs=[pl.BlockSpec((tr, 512), lambda b, l: (row(b, l), 0)),
                   pl.BlockSpec((G, B_HEADS, B_QK, B_VAL), lambda b, l: (b, 0, 0, 0)),
                   pl.BlockSpec((G, B_HEADS, B_QK), lambda b, l: (b, 0, 0)),
                   pl.BlockSpec((G, B_HEADS, 1), lambda b, l: (b, 0, 0))],
        out_shape=[jax.ShapeDtypeStruct((B * L, 512), F32),
                   jax.ShapeDtypeStruct((B, B_HEADS, B_QK, B_VAL), F32),
                   jax.ShapeDtypeStruct((B, B_HEADS, B_QK), F32),
                   jax.ShapeDtypeStruct((B, B_HEADS, 1), F32)],
        scratch_shapes=[pltpu.VMEM((B_HEADS * G, B_QK, B_VAL), F32),
                        pltpu.VMEM((B_HEADS * G, 1, B_QK), F32),
                        pltpu.VMEM((B_HEADS * G * c, 128), F32)],
        compiler_params=pltpu.CompilerParams(
            dimension_semantics=("arbitrary", "arbitrary"), vmem_limit_bytes=VMEM_LIMIT),
        name="mlstm",
    )(proj, proj, proj, proj, bias_row, g, c0, n0, m0)


def _unit_lower_inverse_m1(mats, c, ri, ci):
    if c <= SUB:
        ads, aoffs = mats, None
    else:
        same = (ri // SUB) == (ci // SUB)
        ads = [jnp.where(same, a, 0.0) for a in mats]
        aoffs = [a - ad for a, ad in zip(mats, ads)]
    ts = [-ad for ad in ads]
    ps = ads
    n = 2
    while n < min(c, SUB):
        ps = [_mm(p, p) for p in ps]
        ts = [t + p + _mm(t, p) for t, p in zip(ts, ps)]
        n *= 2
    if aoffs is None:
        return ts
    assert c // SUB == 4
    nns = [aoff + _mm(t, aoff) for t, aoff in zip(ts, aoffs)]
    n2s = [_mm(nn, nn) for nn in nns]
    xs = [t + n2 + _mm(n2, t) for t, n2 in zip(ts, n2s)]
    return [x - nn - _mm(nn, x) for x, nn in zip(xs, nns)]


def _gdn_chunks(r0s, c, G, cv_scr, x_ref, sm_ref, arow_ref, dtrow_ref, g_ref, y_ref, s_scr):
    pre = [_gdn_chunk_pre(r0, c, G, cv_scr, sm_ref, arow_ref, dtrow_ref) for r0 in r0s]
    ri, ci = _iotas(C_HEADS * G * c)
    invs = _unit_lower_inverse_m1([p[0] for p in pre], c, ri, ci)
    for r0, p, u in zip(r0s, pre, invs):
        _gdn_chunk_post(r0, c, G, p, u, x_ref, g_ref, y_ref, s_scr)


def _gdn_chunk_pre(r0, c, G, cv_scr, sm_ref, arow_ref, dtrow_ref):
    n = G * c
    R = C_HEADS * n
    rows = pl.ds(r0, n)
    rn, cn = _iotas(n)
    seq = (rn // c) == (cn // c)
    sm = sm_ref[rows, :]
    beta_all = jax.nn.sigmoid(sm)
    g_all = arow_ref[...] * _softplus(sm + dtrow_ref[...])
    bcum = _mm_hi((seq & (rn >= cn)).astype(F32), g_all)
    brev = _mm_hi((seq & (rn < cn)).astype(F32), g_all)
    stack = lambda f: jnp.concatenate([f(h) for h in range(C_HEADS)], axis=0)
    q = stack(lambda h: cv_scr[rows, h * 128:(h + 1) * 128])
    k = stack(lambda h: cv_scr[rows, 512 + h * 128:512 + (h + 1) * 128])
    v = stack(lambda h: cv_scr[rows, 1024 + h * 128:1024 + (h + 1) * 128])
    beta = stack(lambda h: beta_all[:, 8 + h:9 + h])
    bc = stack(lambda h: bcum[:, 12 + h:13 + h])
    kdec = jnp.exp(stack(lambda h: brev[:, 12 + h:13 + h]))
    q = q * lax.rsqrt(jnp.sum(q * q, -1, keepdims=True) + NORM_EPS) * (C_KEY ** -0.5)
    k = k * lax.rsqrt(jnp.sum(k * k, -1, keepdims=True) + NORM_EPS)

    ri, ci = _iotas(R)
    grp = (ri // c) == (ci // c)
    incl = grp & (ri >= ci)
    strict = grp & (ri > ci)
    diff = bc - _col2row(bc, ri == ci)
    dec = jnp.exp(jnp.where(incl, diff, 0.0))
    a = beta * _mm_nt(k, k) * jnp.where(strict, dec, 0.0)
    ebc = jnp.exp(bc)
    rhs = jnp.concatenate([v * beta, k * (beta * ebc)], axis=-1)
    att = _mm_nt(q, k) * jnp.where(incl, dec, 0.0)
    return a, rhs, att, q * ebc, k * kdec, bcum


def _gdn_chunk_post(r0, c, G, pre, inv_m1, x_ref, g_ref, y_ref, s_scr):
    _, rhs, att, qe, kd, bcum = pre
    n = G * c
    rows = pl.ds(r0, n)
    sol = rhs + _mm(inv_m1, rhs)
    groups = range(C_HEADS * G)
    sls = [slice(j * c, (j + 1) * c) for j in groups]
    s_old = [s_scr[j] for j in groups]
    v_new = [sol[sl, :C_VAL] - _mm(sol[sl, C_VAL:], s) for sl, s in zip(sls, s_old)]
    o_inter = [_mm(qe[sl], s) for sl, s in zip(sls, s_old)]
    for j in groups:
        hh, gg = divmod(j, G)
        bl = bcum[(gg + 1) * c - 1:(gg + 1) * c, 12 + hh:13 + hh]
        s_scr[j] = jnp.exp(bl) * s_old[j] + _mm_tn(kd[sls[j]], v_new[j])
    o = jnp.concatenate(o_inter, axis=0) + _mm(att, jnp.concatenate(v_new, axis=0))
    cg = x_ref[rows, CONV_CH:2048]
    for h in range(C_HEADS):
        sl = slice(h * 128, (h + 1) * 128)
        y_ref[rows, sl] = _rms(o[h * n:(h + 1) * n]) * g_ref[:, sl] * _silu(cg[:, sl])


def _gdn_body(x_ref, sm_ref, cw_ref, arow_ref, dtrow_ref, g_ref, s0_ref, cs_ref,
              y_ref, so_ref, cso_ref, s_scr, xb_scr, cv_scr, *, tl, c, G):
    l = pl.program_id(1)
    args = (cv_scr, x_ref, sm_ref, arow_ref, dtrow_ref, g_ref, y_ref, s_scr)

    @pl.when(l == 0)
    def _():
        for h in range(C_HEADS):
            for g in range(G):
                s_scr[h * G + g] = s0_ref[g, h]

    if G > 1:
        for g in range(G):
            xb_scr[0:8, :] = cs_ref[g]
            xb_scr[8:8 + c, :] = x_ref[g * c:(g + 1) * c, 0:CONV_CH]
            conv = xb_scr[5:5 + c, :] * cw_ref[0:1, :]
            for j in range(1, CONV_W):
                conv = conv + xb_scr[5 + j:5 + j + c, :] * cw_ref[j:j + 1, :]
            cv_scr[g * c:(g + 1) * c, :] = _silu(conv)
            cso_ref[g] = xb_scr[c + 5:c + 8, :]
        _gdn_chunks([0], c, G, *args)
    else:
        @pl.when(l == 0)
        def _():
            xb_scr[0:8, :] = cs_ref[0]

        xb_scr[8:8 + tl, :] = x_ref[:, 0:CONV_CH]
        xb = xb_scr[...]
        conv = xb[8:] * cw_ref[CONV_W - 1:CONV_W, :]
        for j in range(1, CONV_W):
            conv = conv + pltpu.roll(xb, j, 0)[8:] * cw_ref[CONV_W - 1 - j:CONV_W - j, :]
        cv_scr[...] = _silu(conv)

        _gdn_chunks([i * c for i in range(tl // c)], c, 1, *args)

        @pl.when(l == pl.num_programs(1) - 1)
        def _():
            cso_ref[0] = xb_scr[tl + 5:tl + 8, :]

        xb_scr[0:8, :] = xb_scr[tl:tl + 8, :]

    @pl.when(l == pl.num_programs(1) - 1)
    def _():
        for h in range(C_HEADS):
            for g in range(G):
                so_ref[g, h] = s_scr[h * G + g]


def _gdn(proj, B, L, tl, c, G, conv_w, arow, dtrow, g, s0, cs_pad, ls):
    assert G == 1 or (L == tl == c and B % G == 0)
    nl = L // tl
    tr = G * tl
    row = lambda b, l: b * nl + l
    return pl.pallas_call(
        functools.partial(_gdn_body, tl=tl, c=c, G=G),
        grid=(B // G, nl),
        in_specs=[pl.BlockSpec((tr, 2048), lambda b, l: (row(b, l), COL_C // 2048)),
                  pl.BlockSpec((tr, 128), lambda b, l: (row(b, l), COL_S // 128)),
                  pl.BlockSpec((CONV_W, CONV_CH), lambda b, l: (0, 0)),
                  pl.BlockSpec((1, 128), lambda b, l: (0, 0)),
                  pl.BlockSpec((1, 128), lambda b, l: (0, 0)),
                  pl.BlockSpec((1, 512), lambda b, l: (0, 0)),
                  pl.BlockSpec((None, G, C_HEADS, C_KEY, C_VAL), lambda b, l: (ls, b, 0, 0, 0)),
                  pl.BlockSpec((None, G, 8, CONV_CH), lambda b, l: (ls, b, 0, 0))],
        out_specs=[pl.BlockSpec((tr, 512), lambda b, l: (row(b, l), 0)),
                   pl.BlockSpec((G, C_HEADS, C_KEY, C_VAL), lambda b, l: (b, 0, 0, 0)),
                   pl.BlockSpec((G, CONV_W - 1, CONV_CH), lambda b, l: (b, 0, 0))],
        out_shape=[jax.ShapeDtypeStruct((B * L, 512), F32),
                   jax.ShapeDtypeStruct((B, C_HEADS, C_KEY, C_VAL), F32),
                   jax.ShapeDtypeStruct((B, CONV_W - 1, CONV_CH), F32)],
        scratch_shapes=[pltpu.VMEM((C_HEADS * G, C_KEY, C_VAL), F32),
                        pltpu.VMEM((tl + 8, CONV_CH), F32),
                        pltpu.VMEM((tr, CONV_CH), F32)],
        compiler_params=pltpu.CompilerParams(
            dimension_semantics=("arbitrary", "arbitrary"), vmem_limit_bytes=VMEM_LIMIT),
        name="gdn",
    )(proj, proj, conv_w, arow, dtrow, g, s0, cs_pad)


def _pad_row(vals, start):
    return jnp.zeros((1, 128), F32).at[0, start:start + vals.shape[0]].set(vals.astype(F32))


def _layer(h, B, L, tl, c, G, st, ls, w, lw, l):
    st_a, st_c, st_n, st_m, st_g, st_conv = st
    proj = _proj(h, w["w_in"], l)
    ya, na = _hgrn(proj, B, L, tl, c, G, lw["lb"], lw["a_g"], st_a, ls)
    yb, nc, nn, nm = _mlstm(proj, B, L, tl, c, G, lw["b_bias"], lw["b_g"], st_c, st_n, st_m, ls)
    yc, ng, nv = _gdn(proj, B, L, tl, c, G, lw["conv_w"], lw["c_arow"], lw["c_dtrow"], lw["c_g"], st_g, st_conv, ls)
    h = _post(ya, yb, yc, proj, h, w["wb"], w["wo"], w["ln1_g"], w["ln1_b"], w["wfi"], w["wfo"],
              w["ln2_g"], w["ln2_b"], l)
    return h, (na, jnp.swapaxes(nc, -1, -2), nn, nm[..., 0], ng, nv)


def kernel(x_prompt, x_sample, state_hgrn, state_mlstm_c, state_mlstm_n, state_mlstm_m, state_gdn,
           state_gdn_conv, w_in, lb_logits, a_norm_g, b_mi, b_mf, b_norm_g, conv_w, a_log, dt_bias,
           c_norm_g, w_branch, w_out, ln1_g, ln1_b, w_ffn_in, w_ffn_out, ln2_g, ln2_b):
    Bp, Lp, _ = x_prompt.shape
    Bs, Ls, _ = x_sample.shape
    cp, cs = math.gcd(Lp, CHUNK), math.gcd(Ls, CHUNK)
    tlp = min(Lp, 256)

    w = dict(w_in=_reorder_w_in(w_in), wb=w_branch.astype(BF16), wo=w_out.astype(BF16),
             wfi=w_ffn_in.astype(BF16), wfo=w_ffn_out.astype(BF16),
             ln1_g=ln1_g[:, None].astype(F32), ln1_b=ln1_b[:, None].astype(F32),
             ln2_g=ln2_g[:, None].astype(F32), ln2_b=ln2_b[:, None].astype(F32))
    lb_w = jax.nn.softmax(lb_logits.astype(F32), axis=0)
    lb_all = jnp.cumsum(lb_w, axis=0) - lb_w[0]
    hist_pad = ((0, 0), (0, 0), (8 - (CONV_W - 1), 0), (0, 0))

    h_p = x_prompt.reshape(Bp * Lp, D_MODEL)
    h_s = x_sample.reshape(Bs * Ls, D_MODEL)
    st_p = (jnp.zeros((1, Bp, A_HEADS, A_KEY, A_VAL), F32), jnp.zeros((1, Bp, B_HEADS, B_QK, B_VAL), F32),
            jnp.zeros((1, Bp, B_HEADS, B_QK), F32), jnp.zeros((1, Bp, B_HEADS, 1), F32),
            jnp.zeros((1, Bp, C_HEADS, C_KEY, C_VAL), F32), jnp.zeros((1, Bp, 8, CONV_CH), F32))
    st_s = (state_hgrn.astype(F32), jnp.swapaxes(state_mlstm_c.astype(F32), -1, -2), state_mlstm_n.astype(F32),
            state_mlstm_m.astype(F32)[..., None], state_gdn.astype(F32),
            jnp.pad(state_gdn_conv.astype(F32), hist_pad))
    new_p, new_s = [], []
    for l in range(DEPTH):
        lw = dict(lb=lb_all[l][None], a_g=a_norm_g[l][None].astype(F32),
                  b_bias=_pad_row(jnp.concatenate([b_mi[l], b_mf[l]]), 0), b_g=b_norm_g[l][None].astype(F32),
                  conv_w=conv_w[l].astype(F32), c_arow=_pad_row(-jnp.exp(a_log[l].astype(F32)), 12),
                  c_dtrow=_pad_row(dt_bias[l], 12), c_g=c_norm_g[l][None].astype(F32))
        h_p, ns_p = _layer(h_p, Bp, Lp, tlp, cp, 1, st_p, 0, w, lw, l)
        h_s, ns_s = _layer(h_s, Bs, Ls, Ls, cs, SAMPLE_G, st_s, l, w, lw, l)
        new_p.append(ns_p)
        new_s.append(ns_s)

    outs = [h_p.reshape(Bp, Lp, D_MODEL), h_s.reshape(Bs, Ls, D_MODEL)]
    for i, ref in enumerate((state_hgrn, state_mlstm_c, state_mlstm_n, state_mlstm_m, state_gdn,
                             state_gdn_conv)):
        outs.append(jnp.stack([ns[i] for ns in new_p]).astype(ref.dtype))
        outs.append(jnp.stack([ns[i] for ns in new_s]).astype(ref.dtype))
    return tuple(outs)
```

```python
import functools
import math

import jax
import jax.numpy as jnp
from jax import lax
from jax.experimental import pallas as pl
from jax.experimental.pallas import tpu as pltpu

F32 = jnp.float32
BF16 = jnp.bfloat16
MXU_DT = BF16

D_MODEL = 1024
DEPTH = 4
A_HEADS, A_KEY, A_VAL = 4, 128, 128
B_HEADS, B_QK, B_VAL = 4, 64, 128
C_HEADS, C_KEY, C_VAL = 4, 128, 128
CONV_W = 4
CONV_CH = C_HEADS * (2 * C_KEY + C_VAL)
N_BRANCH = 3
BRANCH_W = 512
FFN_HIDDEN = 2816
CHUNK = 64
ALPHA = (2 * DEPTH) ** 0.25
LN_EPS = 1e-5
NORM_EPS = 1e-6
NEG_BIG = -1e30
LOG2E = 1.0 / math.log(2.0)

COL_A = 0
COL_C = 2048
COL_MG = 4096
COL_B = 7168
COL_S = 8704
N_PROJ = 8832
PROJ_TN = N_PROJ // 3

VMEM_LIMIT = 56 * 1024 * 1024


def _reorder_w_in(w_in):
    a = w_in[..., 0:2048]
    b = w_in[..., 2048:3584]
    s1 = w_in[..., 3584:3592]
    c = w_in[..., 3592:5640]
    s2 = w_in[..., 5640:5648]
    mg = w_in[..., 5648:8720]
    pad = jnp.zeros(w_in.shape[:-1] + (N_PROJ - COL_S - 16,), w_in.dtype)
    return jnp.concatenate([a, c, mg, b, s1, s2, pad], axis=-1).astype(BF16)


def _proj_body(x_ref, w_ref, o_ref):
    o_ref[...] = jnp.dot(x_ref[...].astype(BF16), w_ref[...], preferred_element_type=F32)


def _proj(h, w, l):
    T = h.shape[0]
    tm = min(1024, T)
    return pl.pallas_call(
        _proj_body,
        grid=(T // tm, N_PROJ // PROJ_TN),
        in_specs=[pl.BlockSpec((tm, D_MODEL), lambda i, j: (i, 0)),
                  pl.BlockSpec((None, D_MODEL, PROJ_TN), lambda i, j: (l, 0, j))],
        out_specs=pl.BlockSpec((tm, PROJ_TN), lambda i, j: (i, j)),
        out_shape=jax.ShapeDtypeStruct((T, N_PROJ), F32),
        compiler_params=pltpu.CompilerParams(
            dimension_semantics=("arbitrary", "arbitrary"), vmem_limit_bytes=VMEM_LIMIT),
        name="in_proj",
    )(h, w)


def _ln(x, g, b):
    mu = jnp.mean(x, -1, keepdims=True)
    xc = x - mu
    var = jnp.mean(xc * xc, -1, keepdims=True)
    return xc * lax.rsqrt(var + LN_EPS) * g + b


FFN_HC = 256


def _post_body(ya_ref, yb_ref, yc_ref, mg0_ref, mg1_ref, mg2_ref, h_ref, wb_ref, wo_ref, g1_ref, b1_ref,
               wfi_ref, wfo_ref, g2_ref, b2_ref, o_ref, a_scr):
    mgs = (mg0_ref, mg1_ref, mg2_ref)
    ys = (ya_ref, yb_ref, yc_ref)
    merged = None
    for n in range(N_BRANCH):
        y = ys[n][...].astype(BF16)
        z = jnp.dot(y, wb_ref[n], preferred_element_type=F32)
        t = jax.nn.sigmoid(mgs[n][...]) * z
        merged = t if merged is None else merged + t
    mix = jnp.dot(merged.astype(BF16), wo_ref[...], preferred_element_type=F32)
    h1 = _ln(ALPHA * h_ref[...] + mix, g1_ref[...], b1_ref[...])
    h1b = h1.astype(BF16)
    for j in range(FFN_HIDDEN // FFN_HC):
        gate = jnp.dot(h1b, wfi_ref[:, j * FFN_HC:(j + 1) * FFN_HC], preferred_element_type=F32)
        up = jnp.dot(h1b, wfi_ref[:, FFN_HIDDEN + j * FFN_HC:FFN_HIDDEN + (j + 1) * FFN_HC],
                     preferred_element_type=F32)
        a_scr[:, j * FFN_HC:(j + 1) * FFN_HC] = (gate * jax.nn.sigmoid(gate) * up).astype(BF16)
    ff = jnp.dot(a_scr[...], wfo_ref[...], preferred_element_type=F32)
    o_ref[...] = _ln(ALPHA * h1 + ff, g2_ref[...], b2_ref[...])


def _layer_spec(shape, l):
    nd = len(shape)
    return pl.BlockSpec((None,) + shape, lambda i: (l,) + (0,) * nd, pipeline_mode=pl.Buffered(1))


def _post(ya, yb, yc, proj, h, wb, wo, g1, b1, wfi, wfo, g2, b2, l):
    T = h.shape[0]
    tm = min(256, T)
    mg_blk = COL_MG // D_MODEL
    return pl.pallas_call(
        _post_body,
        grid=(T // tm,),
        in_specs=[pl.BlockSpec((tm, BRANCH_W), lambda i: (i, 0)),
                  pl.BlockSpec((tm, BRANCH_W), lambda i: (i, 0)),
                  pl.BlockSpec((tm, BRANCH_W), lambda i: (i, 0)),
                  pl.BlockSpec((tm, D_MODEL), lambda i: (i, mg_blk)),
                  pl.BlockSpec((tm, D_MODEL), lambda i: (i, mg_blk + 1)),
                  pl.BlockSpec((tm, D_MODEL), lambda i: (i, mg_blk + 2)),
                  pl.BlockSpec((tm, D_MODEL), lambda i: (i, 0)),
                  _layer_spec((N_BRANCH, BRANCH_W, D_MODEL), l),
                  _layer_spec((D_MODEL, D_MODEL), l),
                  _layer_spec((1, D_MODEL), l), _layer_spec((1, D_MODEL), l),
                  _layer_spec((D_MODEL, 2 * FFN_HIDDEN), l),
                  _layer_spec((FFN_HIDDEN, D_MODEL), l),
                  _layer_spec((1, D_MODEL), l), _layer_spec((1, D_MODEL), l)],
        out_specs=pl.BlockSpec((tm, D_MODEL), lambda i: (i, 0)),
        out_shape=jax.ShapeDtypeStruct((T, D_MODEL), F32),
        scratch_shapes=[pltpu.VMEM((tm, FFN_HIDDEN), BF16)],
        compiler_params=pltpu.CompilerParams(
            dimension_semantics=("arbitrary",), vmem_limit_bytes=VMEM_LIMIT),
        name="merge_ffn",
    )(ya, yb, yc, proj, proj, proj, h, wb, wo, g1, b1, wfi, wfo, g2, b2)


SAMPLE_G = 8
SUB = 16
SUB_A = 16
DECAY_SAFE = 60.0
_HI = lax.Precision.HIGHEST


def _mm(a, b):
    return jnp.dot(a.astype(MXU_DT), b.astype(MXU_DT), preferred_element_type=F32)


def _mm_nt(a, b):
    return lax.dot_general(a.astype(MXU_DT), b.astype(MXU_DT), (((1,), (1,)), ((), ())),
                           preferred_element_type=F32)


def _mm_tn(a, b):
    return lax.dot_general(a.astype(MXU_DT), b.astype(MXU_DT), (((0,), (0,)), ((), ())),
                           preferred_element_type=F32)


def _mm_hi(a, b):
    return jnp.dot(a, b, precision=_HI, preferred_element_type=F32)


def _silu(x):
    return x * jax.nn.sigmoid(x)


def _softplus(x):
    return jnp.maximum(x, 0.0) + jnp.log1p(jnp.exp(-jnp.abs(x)))


def _masked_exp(mask, t):
    return jnp.where(mask, jnp.exp(jnp.where(mask, t, 0.0)), 0.0)


def _rms(o):
    return o * lax.rsqrt(jnp.mean(o * o, -1, keepdims=True) + NORM_EPS)


def _col2row(col, eye):
    return jnp.sum(jnp.where(eye, col, 0.0), axis=0, keepdims=True)


def _iotas(c):
    ri = lax.broadcasted_iota(jnp.int32, (c, c), 0)
    ci = lax.broadcasted_iota(jnp.int32, (c, c), 1)
    return ri, ci


def _hgrn_prelude(r0, c, a_ref, lb_ref):
    rows = pl.ds(r0, c)
    ri, ci = _iotas(c)
    tri = (ri >= ci).astype(F32)
    lb = lb_ref[...]
    af = a_ref[rows, 512:1024]
    logf = jnp.log(lb + (1.0 - lb) * jax.nn.sigmoid(af))
    kk = (1.0 - lb) * jax.nn.sigmoid(-af)
    b = _mm_hi(tri, logf)
    return _silu(a_ref[rows, 0:512]), kk, a_ref[rows, 1024:1536], b


def _hgrn_intra_exact(q, kk, v, b, c):
    sb = min(SUB_A, c)
    trow = lax.broadcasted_iota(jnp.int32, (sb, 1), 0)
    b2 = b * LOG2E
    intra = []
    for h in range(A_HEADS):
        sl = slice(h * 128, (h + 1) * 128)
        qh, kh, vh, bh = q[:, sl], kk[:, sl], v[:, sl], b[:, sl]
        blocks = []
        for i in range(c // sb):
            rs = slice(i * sb, (i + 1) * sb)
            qi, bi = qh[rs], bh[rs]
            oi = jnp.zeros((sb, A_VAL), F32)
            if i > 0:
                r = bh[i * sb - 1:i * sb]
                att = _mm_nt(qi * jnp.exp(bi - r), kh[:i * sb] * jnp.exp(r - bh[:i * sb]))
                oi = _mm(att, vh[:i * sb])
            ki, vi, b2i = kh[rs], vh[rs], b2[rs, sl]
            for s in range(sb):
                e = jnp.where(trow >= s, jnp.exp2(b2i - b2i[s:s + 1]), 0.0)
                oi = oi + jnp.sum(e * qi * ki[s:s + 1], axis=-1, keepdims=True) * vi[s:s + 1]
            blocks.append(oi)
        intra.append(blocks[0] if len(blocks) == 1 else jnp.concatenate(blocks, axis=0))
    return intra


def _hgrn_intra_bounded(pres, c):
    sb = min(SUB_A, c)
    trow = lax.broadcasted_iota(jnp.int32, (sb, 1), 0)
    jobs = []
    for q, kk, v, b in pres:
        for h in range(A_HEADS):
            sl = slice(h * 128, (h + 1) * 128)
            for i in range(c // sb):
                rs, nk = slice(i * sb, (i + 1) * sb), (i + 1) * sb
                if i == 0:
                    qd, kd = jnp.exp(b[rs, sl]), jnp.exp(-b[:nk, sl])
                else:
                    r = b[i * sb - 1:i * sb, sl]
                    qd, kd = jnp.exp(b[rs, sl] - r), jnp.exp(r - b[:nk, sl])
                jobs.append((i, nk, q[rs, sl] * qd, kk[:nk, sl] * kd, v[:nk, sl]))
    atts = [_mm_nt(lhs, rhs) for _, _, lhs, rhs, _ in jobs]
    atts = [jnp.where(lax.broadcasted_iota(jnp.int32, (sb, nk), 1) <= i * sb + trow, a, 0.0)
            for (i, nk, _, _, _), a in zip(jobs, atts)]
    outs = [_mm(a, vv) for (_, _, _, _, vv), a in zip(jobs, atts)]
    nb = c // sb
    per_head = [outs[j * nb:(j + 1) * nb] for j in range(len(pres) * A_HEADS)]
    per_head = [o[0] if nb == 1 else jnp.concatenate(o, axis=0) for o in per_head]
    return [per_head[ci * A_HEADS:(ci + 1) * A_HEADS] for ci in range(len(pres))]


def _hgrn_decay_span(b, c):
    sb = min(SUB_A, c)
    ends = [b[(i + 1) * sb - 1:(i + 1) * sb] for i in range(c // sb)]
    span = -ends[0]
    for i in range(1, c // sb):
        span = jnp.maximum(span, ends[i - 1] - ends[i])
    return jnp.max(span)


def _hgrn_state_operands(pre, intra, c):
    q, kk, v, b = pre
    bl = b[c - 1:c]
    return intra, q * jnp.exp(b), kk * jnp.exp(bl - b), v, jnp.exp(bl)


def _hgrn_post(units, c, a_ref, g_ref, y_ref, st_scr):
    heads = [slice(h * 128, (h + 1) * 128) for h in range(A_HEADS)]
    jobs = [(r0, g * A_HEADS + h, pre, sl) for r0, g, pre in units for h, sl in enumerate(heads)]
    sts = [st_scr[j] for _, j, _, _ in jobs]
    outs = [pre[0][(j % A_HEADS)] + _mm_nt(pre[1][:, sl], st) for (_, j, pre, sl), st in zip(jobs, sts)]
    for (_, j, pre, sl), st in zip(jobs, sts):
        _, _, kd, v, ebl = pre
        st_scr[j] = st * ebl[:, sl] + _mm_tn(v[:, sl], kd[:, sl])
    for (r0, _, _, sl), o in zip(jobs, outs):
        rows = pl.ds(r0, c)
        y_ref[rows, sl] = _rms(o) * g_ref[:, sl] * _silu(a_ref[rows, 1536 + sl.start:1536 + sl.stop])


def _hgrn_body(a_ref, lb_ref, g_ref, s0_ref, y_ref, so_ref, st_scr, *, c, nchunks, G):
    l = pl.program_id(1)

    @pl.when(l == 0)
    def _():
        for g in range(G):
            for h in range(A_HEADS):
                st_scr[g * A_HEADS + h] = s0_ref[g, h].T

    if G > 1:
        pres = [_hgrn_prelude(g * c, c, a_ref, lb_ref) for g in range(G)]
        units = [(g * c, g, _hgrn_state_operands(p, _hgrn_intra_exact(*p, c), c)) for g, p in enumerate(pres)]
        _hgrn_post(units, c, a_ref, g_ref, y_ref, st_scr)
    else:
        pres = [_hgrn_prelude(i * c, c, a_ref, lb_ref) for i in range(nchunks)]
        span = functools.reduce(jnp.maximum, [_hgrn_decay_span(p[3], c) for p in pres])
        intra = lax.cond(span <= DECAY_SAFE,
                         lambda: _hgrn_intra_bounded(pres, c),
                         lambda: [_hgrn_intra_exact(*p, c) for p in pres])
        for i, p in enumerate(pres):
            _hgrn_post([(i * c, 0, _hgrn_state_operands(p, intra[i], c))], c, a_ref, g_ref, y_ref, st_scr)

    @pl.when(l == pl.num_programs(1) - 1)
    def _():
        for g in range(G):
            for h in range(A_HEADS):
                so_ref[g, h] = st_scr[g * A_HEADS + h].T


def _hgrn(proj, B, L, tl, c, G, lb, g, s0, ls):
    assert G == 1 or (L == tl == c and B % G == 0)
    nl = L // tl
    tr = G * tl
    return pl.pallas_call(
        functools.partial(_hgrn_body, c=c, nchunks=tl // c, G=G),
        grid=(B // G, nl),
        in_specs=[pl.BlockSpec((tr, 2048), lambda b, l: (b * nl + l, COL_A // 2048)),
                  pl.BlockSpec((1, 512), lambda b, l: (0, 0)),
                  pl.BlockSpec((1, 512), lambda b, l: (0, 0)),
                  pl.BlockSpec((None, G, A_HEADS, A_KEY, A_VAL), lambda b, l: (ls, b, 0, 0, 0))],
        out_specs=[pl.BlockSpec((tr, 512), lambda b, l: (b * nl + l, 0)),
                   pl.BlockSpec((G, A_HEADS, A_KEY, A_VAL), lambda b, l: (b, 0, 0, 0))],
        out_shape=[jax.ShapeDtypeStruct((B * L, 512), F32),
                   jax.ShapeDtypeStruct((B, A_HEADS, A_KEY, A_VAL), F32)],
        scratch_shapes=[pltpu.VMEM((G * A_HEADS, A_VAL, A_KEY), F32)],
        compiler_params=pltpu.CompilerParams(
            dimension_semantics=("arbitrary", "arbitrary"), vmem_limit_bytes=VMEM_LIMIT),
        name="hgrn2",
    )(proj, lb, g, s0)


def _mlstm_chunk(r0, c, G, qk_ref, v_ref, og_ref, sm_ref, bias_ref, g_ref, y_ref, c_scr, n_scr, m_scr):
    n = G * c
    R = B_HEADS * n
    rows = pl.ds(r0, n)
    rn, cn = _iotas(n)
    seq = (rn // c) == (cn // c)
    pre = sm_ref[rows, :] + bias_ref[...]
    lf = -_softplus(-pre)
    bcum = _mm_hi((seq & (rn >= cn)).astype(F32), lf)
    brev = _mm_hi((seq & (rn < cn)).astype(F32), lf)
    stack = lambda f: jnp.concatenate([f(h) for h in range(B_HEADS)], axis=0)
    q = stack(lambda h: qk_ref[rows, h * 64:(h + 1) * 64]) * (B_QK ** -0.5)
    k = stack(lambda h: qk_ref[rows, 256 + h * 64:256 + (h + 1) * 64])
    v = stack(lambda h: v_ref[rows, h * 128:(h + 1) * 128])
    ic = stack(lambda h: pre[:, h:h + 1])
    bc = stack(lambda h: bcum[:, 4 + h:5 + h])
    gg = stack(lambda h: brev[:, 4 + h:5 + h]) + ic
    m_st = m_scr[:, 0:1]

    groups = range(B_HEADS * G)
    sls = [slice(j * c, (j + 1) * c) for j in groups]
    cms = [c_scr[j] for j in groups]
    njs = [n_scr[j] for j in groups]
    qc = jnp.concatenate([_mm(q[sl], cm) for sl, cm in zip(sls, cms)], axis=0)
    qn = jnp.concatenate([jnp.sum(q[sl] * nj, -1, keepdims=True) for sl, nj in zip(sls, njs)], axis=0)
    w_olds, wks = [], []
    for j in groups:
        hh, g_ = divmod(j, G)
        m = m_scr[j * c:j * c + 1, 0:1]
        bl = bcum[(g_ + 1) * c - 1:(g_ + 1) * c, 4 + hh:5 + hh]
        m_new = jnp.maximum(bl + m, jnp.max(gg[sls[j]], 0, keepdims=True))
        w_olds.append(jnp.exp(bl + m - m_new))
        wks.append(jnp.exp(gg[sls[j]] - m_new))
        m_scr[sls[j], :] = jnp.broadcast_to(m_new, (c, 128))
    for j in groups:
        c_scr[j] = w_olds[j] * cms[j] + _mm_tn(k[sls[j]], v[sls[j]] * wks[j])
        n_scr[j] = w_olds[j] * njs[j] + jnp.sum(wks[j] * k[sls[j]], 0, keepdims=True)

    ri, ci = _iotas(R)
    incl = ((ri // c) == (ci // c)) & (ri >= ci)
    eye = ri == ci
    logd = jnp.where(incl, bc - _col2row(bc, eye) + _col2row(ic, eye), NEG_BIG)
    m_inter = bc + m_st
    m_t = jnp.maximum(m_inter, jnp.max(logd, -1, keepdims=True))
    w_inter = jnp.exp(m_inter - m_t)
    sc = _mm_nt(q, k) * _masked_exp(incl, logd - m_t)
    num = w_inter * qc + _mm(sc, v)
    den = w_inter * qn + jnp.sum(sc, -1, keepdims=True)
    hh = num / jnp.maximum(jnp.abs(den), jnp.exp(-m_t))
    for h in range(B_HEADS):
        sl = slice(h * 128, (h + 1) * 128)
        y_ref[rows, sl] = _rms(hh[h * n:(h + 1) * n]) * g_ref[:, sl] * jax.nn.sigmoid(og_ref[rows, sl])


def _mlstm_body(qk_ref, v_ref, og_ref, sm_ref, bias_ref, g_ref, c0_ref, n0_ref, m0_ref,
                y_ref, co_ref, no_ref, mo_ref, c_scr, n_scr, m_scr, *, tl, c, G):
    l = pl.program_id(1)

    @pl.when(l == 0)
    def _():
        for h in range(B_HEADS):
            for g in range(G):
                j = h * G + g
                c_scr[j] = c0_ref[g, h]
                n_scr[j] = n0_ref[g, h:h + 1, :]
                m_scr[j * c:(j + 1) * c, :] = jnp.broadcast_to(m0_ref[g, h:h + 1, :], (c, 128))

    args = (qk_ref, v_ref, og_ref, sm_ref, bias_ref, g_ref, y_ref, c_scr, n_scr, m_scr)
    if tl == c:
        _mlstm_chunk(0, c, G, *args)
    else:
        def step(i, carry):
            _mlstm_chunk(pl.multiple_of(i * c, c), c, G, *args)
            return carry
        lax.fori_loop(0, tl // c, step, 0)

    @pl.when(l == pl.num_programs(1) - 1)
    def _():
        for h in range(B_HEADS):
            for g in range(G):
                j = h * G + g
                co_ref[g, h] = c_scr[j]
                no_ref[g, h:h + 1, :] = n_scr[j]
                mo_ref[g, h:h + 1, :] = m_scr[j * c:j * c + 1, 0:1]


def _mlstm(proj, B, L, tl, c, G, bias_row, g, c0, n0, m0, ls):
    assert G == 1 or (L == tl == c and B % G == 0)
    nl = L // tl
    tr = G * tl
    cb = COL_B // 512
    row = lambda b, l: b * nl + l
    return pl.pallas_call(
        functools.partial(_mlstm_body, tl=tl, c=c, G=G),
        grid=(B // G, nl),
        in_specs=[pl.BlockSpec((tr, 512), lambda b, l: (row(b, l), cb)),
                  pl.BlockSpec((tr, 512), lambda b, l: (row(b, l), cb + 1)),
                  pl.BlockSpec((tr, 512), lambda b, l: (row(b, l), cb + 2)),
                  pl.BlockSpec((tr, 128), lambda b, l: (row(b, l), COL_S // 128)),
                  pl.BlockSpec((1, 128), lambda b, l: (0, 0)),
                  pl.BlockSpec((1, 512), lambda b, l: (0, 0)),
                  pl.BlockSpec((None, G, B_HEADS, B_QK, B_VAL), lambda b, l: (ls, b, 0, 0, 0)),
                  pl.BlockSpec((None, G, B_HEADS, B_QK), lambda b, l: (ls, b, 0, 0)),
                  pl.BlockSpec((None, G, B_HEADS, 1), lambda b, l: (ls, b, 0, 0))],
        out_specs=[pl.BlockSpec((tr, 512), lambda b, l: (row(b, l), 0)),
                   pl.BlockSpec((G, B_HEADS, B_QK, B_VAL), lambda b, l: (b, 0, 0, 0)),
                   pl.BlockSpec((G, B_HEADS, B_QK), lambda b, l: (b, 0, 0)),
                   pl.BlockSpec((G, B_HEADS, 1), lambda b, l: (b, 0, 0))],
        out_shape=[jax.ShapeDtypeStruct((B * L, 512), F32),
                   jax.ShapeDtypeStruct((B, B_HEADS, B_QK, B_VAL), F32),
                   jax.ShapeDtypeStruct((B, B_HEADS, B_QK), F32),
                   jax.ShapeDtypeStruct((B, B_HEADS, 1), F32)],
        scratch_shapes=[pltpu.VMEM((B_HEADS * G, B_QK, B_VAL), F32),
                        pltpu.VMEM((B_HEADS * G, 1, B_QK), F32),
                        pltpu.VMEM((B_HEADS * G * c, 128), F32)],
        compiler_params=pltpu.CompilerParams(
            dimension_semantics=("arbitrary", "arbitrary"), vmem_limit_bytes=VMEM_LIMIT),
        name="mlstm",
    )(proj, proj, proj, proj, bias_row, g, c0, n0, m0)


def _unit_lower_inverse_m1(mats, c, ri, ci):
    if c <= SUB:
        ads, aoffs = mats, None
    else:
        same = (ri // SUB) == (ci // SUB)
        ads = [jnp.where(same, a, 0.0) for a in mats]
        aoffs = [a - ad for a, ad in zip(mats, ads)]
    ts = [-ad for ad in ads]
    ps = ads
    n = 2
    while n < min(c, SUB):
        ps = [_mm(p, p) for p in ps]
        ts = [t + p + _mm(t, p) for t, p in zip(ts, ps)]
        n *= 2
    if aoffs is None:
        return ts
    assert c // SUB == 4
    nns = [aoff + _mm(t, aoff) for t, aoff in zip(ts, aoffs)]
    n2s = [_mm(nn, nn) for nn in nns]
    xs = [t + n2 + _mm(n2, t) for t, n2 in zip(ts, n2s)]
    return [x - nn - _mm(nn, x) for x, nn in zip(xs, nns)]


def _gdn_chunks(r0s, c, G, cv_scr, x_ref, sm_ref, arow_ref, dtrow_ref, g_ref, y_ref, s_scr):
    pre = [_gdn_chunk_pre(r0, c, G, cv_scr, sm_ref, arow_ref, dtrow_ref) for r0 in r0s]
    ri, ci = _iotas(C_HEADS * G * c)
    invs = _unit_lower_inverse_m1([p[0] for p in pre], c, ri, ci)
    for r0, p, u in zip(r0s, pre, invs):
        _gdn_chunk_post(r0, c, G, p, u, x_ref, g_ref, y_ref, s_scr)


def _gdn_chunk_pre(r0, c, G, cv_scr, sm_ref, arow_ref, dtrow_ref):
    n = G * c
    R = C_HEADS * n
    rows = pl.ds(r0, n)
    rn, cn = _iotas(n)
    seq = (rn // c) == (cn // c)
    sm = sm_ref[rows, :]
    beta_all = jax.nn.sigmoid(sm)
    g_all = arow_ref[...] * _softplus(sm + dtrow_ref[...])
    bcum = _mm_hi((seq & (rn >= cn)).astype(F32), g_all)
    brev = _mm_hi((seq & (rn < cn)).astype(F32), g_all)
    stack = lambda f: jnp.concatenate([f(h) for h in range(C_HEADS)], axis=0)
    q = stack(lambda h: cv_scr[rows, h * 128:(h + 1) * 128])
    k = stack(lambda h: cv_scr[rows, 512 + h * 128:512 + (h + 1) * 128])
    v = stack(lambda h: cv_scr[rows, 1024 + h * 128:1024 + (h + 1) * 128])
    beta = stack(lambda h: beta_all[:, 8 + h:9 + h])
    bc = stack(lambda h: bcum[:, 12 + h:13 + h])
    kdec = jnp.exp(stack(lambda h: brev[:, 12 + h:13 + h]))
    q = q * lax.rsqrt(jnp.sum(q * q, -1, keepdims=True) + NORM_EPS) * (C_KEY ** -0.5)
    k = k * lax.rsqrt(jnp.sum(k * k, -1, keepdims=True) + NORM_EPS)

    ri, ci = _iotas(R)
    grp = (ri // c) == (ci // c)
    incl = grp & (ri >= ci)
    strict = grp & (ri > ci)
    diff = bc - _col2row(bc, ri == ci)
    dec = jnp.exp(jnp.where(incl, diff, 0.0))
    a = beta * _mm_nt(k, k) * jnp.where(strict, dec, 0.0)
    ebc = jnp.exp(bc)
    rhs = jnp.concatenate([v * beta, k * (beta * ebc)], axis=-1)
    att = _mm_nt(q, k) * jnp.where(incl, dec, 0.0)
    return a, rhs, att, q * ebc, k * kdec, bcum


def _gdn_chunk_post(r0, c, G, pre, inv_m1, x_ref, g_ref, y_ref, s_scr):
    _, rhs, att, qe, kd, bcum = pre
    n = G * c
    rows = pl.ds(r0, n)
    sol = rhs + _mm(inv_m1, rhs)
    groups = range(C_HEADS * G)
    sls = [slice(j * c, (j + 1) * c) for j in groups]
    s_old = [s_scr[j] for j in groups]
    v_new = [sol[sl, :C_VAL] - _mm(sol[sl, C_VAL:], s) for sl, s in zip(sls, s_old)]
    o_inter = [_mm(qe[sl], s) for sl, s in zip(sls, s_old)]
    for j in groups:
        hh, gg = divmod(j, G)
        bl = bcum[(gg + 1) * c - 1:(gg + 1) * c, 12 + hh:13 + hh]
        s_scr[j] = jnp.exp(bl) * s_old[j] + _mm_tn(kd[sls[j]], v_new[j])
    o = jnp.concatenate(o_inter, axis=0) + _mm(att, jnp.concatenate(v_new, axis=0))
    cg = x_ref[rows, CONV_CH:2048]
    for h in range(C_HEADS):
        sl = slice(h * 128, (h + 1) * 128)
        y_ref[rows, sl] = _rms(o[h * n:(h + 1) * n]) * g_ref[:, sl] * _silu(cg[:, sl])


def _gdn_body(x_ref, sm_ref, cw_ref, arow_ref, dtrow_ref, g_ref, s0_ref, cs_ref,
              y_ref, so_ref, cso_ref, s_scr, xb_scr, cv_scr, *, tl, c, G):
    l = pl.program_id(1)
    args = (cv_scr, x_ref, sm_ref, arow_ref, dtrow_ref, g_ref, y_ref, s_scr)

    @pl.when(l == 0)
    def _():
        for h in range(C_HEADS):
            for g in range(G):
                s_scr[h * G + g] = s0_ref[g, h]

    if G > 1:
        for g in range(G):
            xb_scr[0:8, :] = cs_ref[g]
            xb_scr[8:8 + c, :] = x_ref[g * c:(g + 1) * c, 0:CONV_CH]
            conv = xb_scr[5:5 + c, :] * cw_ref[0:1, :]
            for j in range(1, CONV_W):
                conv = conv + xb_scr[5 + j:5 + j + c, :] * cw_ref[j:j + 1, :]
            cv_scr[g * c:(g + 1) * c, :] = _silu(conv)
            cso_ref[g] = xb_scr[c + 5:c + 8, :]
        _gdn_chunks([0], c, G, *args)
    else:
        @pl.when(l == 0)
        def _():
            xb_scr[0:8, :] = cs_ref[0]

        xb_scr[8:8 + tl, :] = x_ref[:, 0:CONV_CH]
        xb = xb_scr[...]
        conv = xb[8:] * cw_ref[CONV_W - 1:CONV_W, :]
        for j in range(1, CONV_W):
            conv = conv + pltpu.roll(xb, j, 0)[8:] * cw_ref[CONV_W - 1 - j:CONV_W - j, :]
        cv_scr[...] = _silu(conv)

        _gdn_chunks([i * c for i in range(tl // c)], c, 1, *args)

        @pl.when(l == pl.num_programs(1) - 1)
        def _():
            cso_ref[0] = xb_scr[tl + 5:tl + 8, :]

        xb_scr[0:8, :] = xb_scr[tl:tl + 8, :]

    @pl.when(l == pl.num_programs(1) - 1)
    def _():
        for h in range(C_HEADS):
            for g in range(G):
                so_ref[g, h] = s_scr[h * G + g]


def _gdn(proj, B, L, tl, c, G, conv_w, arow, dtrow, g, s0, cs_pad, ls):
    assert G == 1 or (L == tl == c and B % G == 0)
    nl = L // tl
    tr = G * tl
    row = lambda b, l: b * nl + l
    return pl.pallas_call(
        functools.partial(_gdn_body, tl=tl, c=c, G=G),
        grid=(B // G, nl),
        in_specs=[pl.BlockSpec((tr, 2048), lambda b, l: (row(b, l), COL_C // 2048)),
                  pl.BlockSpec((tr, 128), lambda b, l: (row(b, l), COL_S // 128)),
                  pl.BlockSpec((CONV_W, CONV_CH), lambda b, l: (0, 0)),
                  pl.BlockSpec((1, 128), lambda b, l: (0, 0)),
                  pl.BlockSpec((1, 128), lambda b, l: (0, 0)),
                  pl.BlockSpec((1, 512), lambda b, l: (0, 0)),
                  pl.BlockSpec((None, G, C_HEADS, C_KEY, C_VAL), lambda b, l: (ls, b, 0, 0, 0)),
                  pl.BlockSpec((None, G, 8, CONV_CH), lambda b, l: (ls, b, 0, 0))],
        out_specs=[pl.BlockSpec((tr, 512), lambda b, l: (row(b, l), 0)),
                   pl.BlockSpec((G, C_HEADS, C_KEY, C_VAL), lambda b, l: (b, 0, 0, 0)),
                   pl.BlockSpec((G, CONV_W - 1, CONV_CH), lambda b, l: (b, 0, 0))],
        out_shape=[jax.ShapeDtypeStruct((B * L, 512), F32),
                   jax.ShapeDtypeStruct((B, C_HEADS, C_KEY, C_VAL), F32),
                   jax.ShapeDtypeStruct((B, CONV_W - 1, CONV_CH), F32)],
        scratch_shapes=[pltpu.VMEM((C_HEADS * G, C_KEY, C_VAL), F32),
                        pltpu.VMEM((tl + 8, CONV_CH), F32),
                        pltpu.VMEM((tr, CONV_CH), F32)],
        compiler_params=pltpu.CompilerParams(
            dimension_semantics=("arbitrary", "arbitrary"), vmem_limit_bytes=VMEM_LIMIT),
        name="gdn",
    )(proj, proj, conv_w, arow, dtrow, g, s0, cs_pad)


def _pad_row(vals, start):
    return jnp.zeros((1, 128), F32).at[0, start:start + vals.shape[0]].set(vals.astype(F32))


def _layer(h, B, L, tl, c, G, st, ls, w, lw, l):
    st_a, st_c, st_n, st_m, st_g, st_conv = st
    proj = _proj(h, w["w_in"], l)
    ya, na = _hgrn(proj, B, L, tl, c, G, lw["lb"], lw["a_g"], st_a, ls)
    yb, nc, nn, nm = _mlstm(proj, B, L, tl, c, G, lw["b_bias"], lw["b_g"], st_c, st_n, st_m, ls)
    yc, ng, nv = _gdn(proj, B, L, tl, c, G, lw["conv_w"], lw["c_arow"], lw["c_dtrow"], lw["c_g"], st_g, st_conv, ls)
    h = _post(ya, yb, yc, proj, h, w["wb"], w["wo"], w["ln1_g"], w["ln1_b"], w["wfi"], w["wfo"],
              w["ln2_g"], w["ln2_b"], l)
    return h, (na, jnp.swapaxes(nc, -1, -2), nn, nm[..., 0], ng, nv)


def kernel(x_prompt, x_sample, state_hgrn, state_mlstm_c, state_mlstm_n, state_mlstm_m, state_gdn,
           state_gdn_conv, w_in, lb_logits, a_norm_g, b_mi, b_mf, b_norm_g, conv_w, a_log, dt_bias,
           c_norm_g, w_branch, w_out, ln1_g, ln1_b, w_ffn_in, w_ffn_out, ln2_g, ln2_b):
    Bp, Lp, _ = x_prompt.shape
    Bs, Ls, _ = x_sample.shape
    cp, cs = math.gcd(Lp, CHUNK), math.gcd(Ls, CHUNK)
    tlp = min(Lp, 256)

    w = dict(w_in=_reorder_w_in(w_in), wb=w_branch.astype(BF16), wo=w_out.astype(BF16),
             wfi=w_ffn_in.astype(BF16), wfo=w_ffn_out.astype(BF16),
             ln1_g=ln1_g[:, None].astype(F32), ln1_b=ln1_b[:, None].astype(F32),
             ln2_g=ln2_g[:, None].astype(F32), ln2_b=ln2_b[:, None].astype(F32))
    lb_w = jax.nn.softmax(lb_logits.astype(F32), axis=0)
    lb_all = jnp.cumsum(lb_w, axis=0) - lb_w[0]
    hist_pad = ((0, 0), (0, 0), (8 - (CONV_W - 1), 0), (0, 0))

    h_p = x_prompt.reshape(Bp * Lp, D_MODEL)
    h_s = x_sample.reshape(Bs * Ls, D_MODEL)
    st_p = (jnp.zeros((1, Bp, A_HEADS, A_KEY, A_VAL), F32), jnp.zeros((1, Bp, B_HEADS, B_QK, B_VAL), F32),
            jnp.zeros((1, Bp, B_HEADS, B_QK), F32), jnp.zeros((1, Bp, B_HEADS, 1), F32),
            jnp.zeros((1, Bp, C_HEADS, C_KEY, C_VAL), F32), jnp.zeros((1, Bp, 8, CONV_CH), F32))
    st_s = (state_hgrn.astype(F32), jnp.swapaxes(state_mlstm_c.astype(F32), -1, -2), state_mlstm_n.astype(F32),
            state_mlstm_m.astype(F32)[..., None], state_gdn.astype(F32),
            jnp.pad(state_gdn_conv.astype(F32), hist_pad))
    new_p, new_s = [], []
    for l in range(DEPTH):
        lw = dict(lb=lb_all[l][None], a_g=a_norm_g[l][None].astype(F32),
                  b_bias=_pad_row(jnp.concatenate([b_mi[l], b_mf[l]]), 0), b_g=b_norm_g[l][None].astype(F32),
                  conv_w=conv_w[l].astype(F32), c_arow=_pad_row(-jnp.exp(a_log[l].astype(F32)), 12),
                  c_dtrow=_pad_row(dt_bias[l], 12), c_g=c_norm_g[l][None].astype(F32))
        h_p, ns_p = _layer(h_p, Bp, Lp, tlp, cp, 1, st_p, 0, w, lw, l)
        h_s, ns_s = _layer(h_s, Bs, Ls, Ls, cs, SAMPLE_G, st_s, l, w, lw, l)
        new_p.append(ns_p)
        new_s.append(ns_s)

    outs = [h_p.reshape(Bp, Lp, D_MODEL), h_s.reshape(Bs, Ls, D_MODEL)]
    for i, ref in enumerate((state_hgrn, state_mlstm_c, state_mlstm_n, state_mlstm_m, state_gdn,
                             state_gdn_conv)):
        outs.append(jnp.stack([ns[i] for ns in new_p]).astype(ref.dtype))
        outs.append(jnp.stack([ns[i] for ns in new_s]).astype(ref.dtype))
    return tuple(outs)
```

```python
import functools
import math

import jax
import jax.numpy as jnp
from jax import lax
from jax.experimental import pallas as pl
from jax.experimental.pallas import tpu as pltpu

F32 = jnp.float32
BF16 = jnp.bfloat16
MXU_DT = BF16

D_MODEL = 1024
DEPTH = 4
A_HEADS, A_KEY, A_VAL = 4, 128, 128
B_HEADS, B_QK, B_VAL = 4, 64, 128
C_HEADS, C_KEY, C_VAL = 4, 128, 128
CONV_W = 4
CONV_CH = C_HEADS * (2 * C_KEY + C_VAL)
N_BRANCH = 3
BRANCH_W = 512
FFN_HIDDEN = 2816
CHUNK = 64
ALPHA = (2 * DEPTH) ** 0.25
LN_EPS = 1e-5
NORM_EPS = 1e-6
NEG_BIG = -1e30
LOG2E = 1.0 / math.log(2.0)

COL_A = 0
COL_C = 2048
COL_MG = 4096
COL_B = 7168
COL_S = 8704
N_PROJ = 8832
PROJ_TN = N_PROJ // 3

VMEM_LIMIT = 56 * 1024 * 1024


def _reorder_w_in(w_in):
    a = w_in[..., 0:2048]
    b = w_in[..., 2048:3584]
    s1 = w_in[..., 3584:3592]
    c = w_in[..., 3592:5640]
    s2 = w_in[..., 5640:5648]
    mg = w_in[..., 5648:8720]
    pad = jnp.zeros(w_in.shape[:-1] + (N_PROJ - COL_S - 16,), w_in.dtype)
    return jnp.concatenate([a, c, mg, b, s1, s2, pad], axis=-1).astype(BF16)


def _proj_body(x_ref, w_ref, o_ref):
    o_ref[...] = jnp.dot(x_ref[...].astype(BF16), w_ref[...], preferred_element_type=F32)


def _proj(h, w, l):
    T = h.shape[0]
    tm = min(1024, T)
    return pl.pallas_call(
        _proj_body,
        grid=(T // tm, N_PROJ // PROJ_TN),
        in_specs=[pl.BlockSpec((tm, D_MODEL), lambda i, j: (i, 0)),
                  pl.BlockSpec((None, D_MODEL, PROJ_TN), lambda i, j: (l, 0, j))],
        out_specs=pl.BlockSpec((tm, PROJ_TN), lambda i, j: (i, j)),
        out_shape=jax.ShapeDtypeStruct((T, N_PROJ), F32),
        compiler_params=pltpu.CompilerParams(
            dimension_semantics=("arbitrary", "arbitrary"), vmem_limit_bytes=VMEM_LIMIT),
        name="in_proj",
    )(h, w)


def _ln(x, g, b):
    mu = jnp.mean(x, -1, keepdims=True)
    xc = x - mu
    var = jnp.mean(xc * xc, -1, keepdims=True)
    return xc * lax.rsqrt(var + LN_EPS) * g + b


FFN_HC = 256


def _post_body(ya_ref, yb_ref, yc_ref, mg0_ref, mg1_ref, mg2_ref, h_ref, wb_ref, wo_ref, g1_ref, b1_ref,
               wfi_ref, wfo_ref, g2_ref, b2_ref, o_ref, a_scr):
    mgs = (mg0_ref, mg1_ref, mg2_ref)
    ys = (ya_ref, yb_ref, yc_ref)
    merged = None
    for n in range(N_BRANCH):
        y = ys[n][...].astype(BF16)
        z = jnp.dot(y, wb_ref[n], preferred_element_type=F32)
        t = jax.nn.sigmoid(mgs[n][...]) * z
        merged = t if merged is None else merged + t
    mix = jnp.dot(merged.astype(BF16), wo_ref[...], preferred_element_type=F32)
    h1 = _ln(ALPHA * h_ref[...] + mix, g1_ref[...], b1_ref[...])
    h1b = h1.astype(BF16)
    for j in range(FFN_HIDDEN // FFN_HC):
        gate = jnp.dot(h1b, wfi_ref[:, j * FFN_HC:(j + 1) * FFN_HC], preferred_element_type=F32)
        up = jnp.dot(h1b, wfi_ref[:, FFN_HIDDEN + j * FFN_HC:FFN_HIDDEN + (j + 1) * FFN_HC],
                     preferred_element_type=F32)
        a_scr[:, j * FFN_HC:(j + 1) * FFN_HC] = (gate * jax.nn.sigmoid(gate) * up).astype(BF16)
    ff = jnp.dot(a_scr[...], wfo_ref[...], preferred_element_type=F32)
    o_ref[...] = _ln(ALPHA * h1 + ff, g2_ref[...], b2_ref[...])


def _layer_spec(shape, l):
    nd = len(shape)
    return pl.BlockSpec((None,) + shape, lambda i: (l,) + (0,) * nd, pipeline_mode=pl.Buffered(1))


def _post(ya, yb, yc, proj, h, wb, wo, g1, b1, wfi, wfo, g2, b2, l):
    T = h.shape[0]
    tm = min(256, T)
    mg_blk = COL_MG // D_MODEL
    return pl.pallas_call(
        _post_body,
        grid=(T // tm,),
        in_specs=[pl.BlockSpec((tm, BRANCH_W), lambda i: (i, 0)),
                  pl.BlockSpec((tm, BRANCH_W), lambda i: (i, 0)),
                  pl.BlockSpec((tm, BRANCH_W), lambda i: (i, 0)),
                  pl.BlockSpec((tm, D_MODEL), lambda i: (i, mg_blk)),
                  pl.BlockSpec((tm, D_MODEL), lambda i: (i, mg_blk + 1)),
                  pl.BlockSpec((tm, D_MODEL), lambda i: (i, mg_blk + 2)),
                  pl.BlockSpec((tm, D_MODEL), lambda i: (i, 0)),
                  _layer_spec((N_BRANCH, BRANCH_W, D_MODEL), l),
                  _layer_spec((D_MODEL, D_MODEL), l),
                  _layer_spec((1, D_MODEL), l), _layer_spec((1, D_MODEL), l),
                  _layer_spec((D_MODEL, 2 * FFN_HIDDEN), l),
                  _layer_spec((FFN_HIDDEN, D_MODEL), l),
                  _layer_spec((1, D_MODEL), l), _layer_spec((1, D_MODEL), l)],
        out_specs=pl.BlockSpec((tm, D_MODEL), lambda i: (i, 0)),
        out_shape=jax.ShapeDtypeStruct((T, D_MODEL), F32),
        scratch_shapes=[pltpu.VMEM((tm, FFN_HIDDEN), BF16)],
        compiler_params=pltpu.CompilerParams(
            dimension_semantics=("arbitrary",), vmem_limit_bytes=VMEM_LIMIT),
        name="merge_ffn",
    )(ya, yb, yc, proj, proj, proj, h, wb, wo, g1, b1, wfi, wfo, g2, b2)


SAMPLE_G = 8
SUB = 16
SUB_A = 16
DECAY_SAFE = 60.0
_HI = lax.Precision.HIGHEST


def _mm(a, b):
    return jnp.dot(a.astype(MXU_DT), b.astype(MXU_DT), preferred_element_type=F32)


def _mm_nt(a, b):
    return lax.dot_general(a.astype(MXU_DT), b.astype(MXU_DT), (((1,), (1,)), ((), ())),
                           preferred_element_type=F32)


def _mm_tn(a, b):
    return lax.dot_general(a.astype(MXU_DT), b.astype(MXU_DT), (((0,), (0,)), ((), ())),
                           preferred_element_type=F32)


def _mm_hi(a, b):
    return jnp.dot(a, b, precision=_HI, preferred_element_type=F32)


def _silu(x):
    return x * jax.nn.sigmoid(x)


def _softplus(x):
    return jnp.maximum(x, 0.0) + jnp.log1p(jnp.exp(-jnp.abs(x)))


def _masked_exp(mask, t):
    return jnp.where(mask, jnp.exp(jnp.where(mask, t, 0.0)), 0.0)


def _rms(o):
    return o * lax.rsqrt(jnp.mean(o * o, -1, keepdims=True) + NORM_EPS)


def _col2row(col, eye):
    return jnp.sum(jnp.where(eye, col, 0.0), axis=0, keepdims=True)


def _iotas(c):
    ri = lax.broadcasted_iota(jnp.int32, (c, c), 0)
    ci = lax.broadcasted_iota(jnp.int32, (c, c), 1)
    return ri, ci


def _hgrn_prelude(r0, c, a_ref, lb_ref):
    rows = pl.ds(r0, c)
    ri, ci = _iotas(c)
    tri = (ri >= ci).astype(F32)
    lb = lb_ref[...]
    af = a_ref[rows, 512:1024]
    logf = jnp.log(lb + (1.0 - lb) * jax.nn.sigmoid(af))
    kk = (1.0 - lb) * jax.nn.sigmoid(-af)
    b = _mm_hi(tri, logf)
    return _silu(a_ref[rows, 0:512]), kk, a_ref[rows, 1024:1536], b


def _hgrn_intra_exact(q, kk, v, b, c):
    sb = min(SUB_A, c)
    trow = lax.broadcasted_iota(jnp.int32, (sb, 1), 0)
    b2 = b * LOG2E
    intra = []
    for h in range(A_HEADS):
        sl = slice(h * 128, (h + 1) * 128)
        qh, kh, vh, bh = q[:, sl], kk[:, sl], v[:, sl], b[:, sl]
        blocks = []
        for i in range(c // sb):
            rs = slice(i * sb, (i + 1) * sb)
            qi, bi = qh[rs], bh[rs]
            oi = jnp.zeros((sb, A_VAL), F32)
            if i > 0:
                r = bh[i * sb - 1:i * sb]
                att = _mm_nt(qi * jnp.exp(bi - r), kh[:i * sb] * jnp.exp(r - bh[:i * sb]))
                oi = _mm(att, vh[:i * sb])
            ki, vi, b2i = kh[rs], vh[rs], b2[rs, sl]
            for s in range(sb):
                e = jnp.where(trow >= s, jnp.exp2(b2i - b2i[s:s + 1]), 0.0)
                oi = oi + jnp.sum(e * qi * ki[s:s + 1], axis=-1, keepdims=True) * vi[s:s + 1]
            blocks.append(oi)
        intra.append(blocks[0] if len(blocks) == 1 else jnp.concatenate(blocks, axis=0))
    return intra


def _hgrn_intra_bounded(pres, c):
    sb = min(SUB_A, c)
    trow = lax.broadcasted_iota(jnp.int32, (sb, 1), 0)
    jobs = []
    for q, kk, v, b in pres:
        for h in range(A_HEADS):
            sl = slice(h * 128, (h + 1) * 128)
            for i in range(c // sb):
                rs, nk = slice(i * sb, (i + 1) * sb), (i + 1) * sb
                if i == 0:
                    qd, kd = jnp.exp(b[rs, sl]), jnp.exp(-b[:nk, sl])
                else:
                    r = b[i * sb - 1:i * sb, sl]
                    qd, kd = jnp.exp(b[rs, sl] - r), jnp.exp(r - b[:nk, sl])
                jobs.append((i, nk, q[rs, sl] * qd, kk[:nk, sl] * kd, v[:nk, sl]))
    atts = [_mm_nt(lhs, rhs) for _, _, lhs, rhs, _ in jobs]
    atts = [jnp.where(lax.broadcasted_iota(jnp.int32, (sb, nk), 1) <= i * sb + trow, a, 0.0)
            for (i, nk, _, _, _), a in zip(jobs, atts)]
    outs = [_mm(a, vv) for (_, _, _, _, vv), a in zip(jobs, atts)]
    nb = c // sb
    per_head = [outs[j * nb:(j + 1) * nb] for j in range(len(pres) * A_HEADS)]
    per_head = [o[0] if nb == 1 else jnp.concatenate(o, axis=0) for o in per_head]
    return [per_head[ci * A_HEADS:(ci + 1) * A_HEADS] for ci in range(len(pres))]


def _hgrn_decay_span(b, c):
    sb = min(SUB_A, c)
    ends = [b[(i + 1) * sb - 1:(i + 1) * sb] for i in range(c // sb)]
    span = -ends[0]
    for i in range(1, c // sb):
        span = jnp.maximum(span, ends[i - 1] - ends[i])
    return jnp.max(span)


def _hgrn_state_operands(pre, intra, c):
    q, kk, v, b = pre
    bl = b[c - 1:c]
    return intra, q * jnp.exp(b), kk * jnp.exp(bl - b), v, jnp.exp(bl)


def _hgrn_post(units, c, a_ref, g_ref, y_ref, st_scr):
    heads = [slice(h * 128, (h + 1) * 128) for h in range(A_HEADS)]
    jobs = [(r0, g * A_HEADS + h, pre, sl) for r0, g, pre in units for h, sl in enumerate(heads)]
    sts = [st_scr[j] for _, j, _, _ in jobs]
    outs = [pre[0][(j % A_HEADS)] + _mm_nt(pre[1][:, sl], st) for (_, j, pre, sl), st in zip(jobs, sts)]
    for (_, j, pre, sl), st in zip(jobs, sts):
        _, _, kd, v, ebl = pre
        st_scr[j] = st * ebl[:, sl] + _mm_tn(v[:, sl], kd[:, sl])
    for (r0, _, _, sl), o in zip(jobs, outs):
        rows = pl.ds(r0, c)
        y_ref[rows, sl] = _rms(o) * g_ref[:, sl] * _silu(a_ref[rows, 1536 + sl.start:1536 + sl.stop])


def _hgrn_body(a_ref, lb_ref, g_ref, s0_ref, y_ref, so_ref, st_scr, *, c, nchunks, G):
    l = pl.program_id(1)

    @pl.when(l == 0)
    def _():
        for g in range(G):
            for h in range(A_HEADS):
                st_scr[g * A_HEADS + h] = s0_ref[g, h].T

    if G > 1:
        pres = [_hgrn_prelude(g * c, c, a_ref, lb_ref) for g in range(G)]
        units = [(g * c, g, _hgrn_state_operands(p, _hgrn_intra_exact(*p, c), c)) for g, p in enumerate(pres)]
        _hgrn_post(units, c, a_ref, g_ref, y_ref, st_scr)
    else:
        pres = [_hgrn_prelude(i * c, c, a_ref, lb_ref) for i in range(nchunks)]
        span = functools.reduce(jnp.maximum, [_hgrn_decay_span(p[3], c) for p in pres])
        intra = lax.cond(span <= DECAY_SAFE,
                         lambda: _hgrn_intra_bounded(pres, c),
                         lambda: [_hgrn_intra_exact(*p, c) for p in pres])
        for i, p in enumerate(pres):
            _hgrn_post([(i * c, 0, _hgrn_state_operands(p, intra[i], c))], c, a_ref, g_ref, y_ref, st_scr)

    @pl.when(l == pl.num_programs(1) - 1)
    def _():
        for g in range(G):
            for h in range(A_HEADS):
                so_ref[g, h] = st_scr[g * A_HEADS + h].T


def _hgrn(proj, B, L, tl, c, G, lb, g, s0, ls):
    assert G == 1 or (L == tl == c and B % G == 0)
    nl = L // tl
    tr = G * tl
    return pl.pallas_call(
        functools.partial(_hgrn_body, c=c, nchunks=tl // c, G=G),
        grid=(B // G, nl),
        in_specs=[pl.BlockSpec((tr, 2048), lambda b, l: (b * nl + l, COL_A // 2048)),
                  pl.BlockSpec((1, 512), lambda b, l: (0, 0)),
                  pl.BlockSpec((1, 512), lambda b, l: (0, 0)),
                  pl.BlockSpec((None, G, A_HEADS, A_KEY, A_VAL), lambda b, l: (ls, b, 0, 0, 0))],
        out_specs=[pl.BlockSpec((tr, 512), lambda b, l: (b * nl + l, 0)),
                   pl.BlockSpec((G, A_HEADS, A_KEY, A_VAL), lambda b, l: (b, 0, 0, 0))],
        out_shape=[jax.ShapeDtypeStruct((B * L, 512), F32),
                   jax.ShapeDtypeStruct((B, A_HEADS, A_KEY, A_VAL), F32)],
        scratch_shapes=[pltpu.VMEM((G * A_HEADS, A_VAL, A_KEY), F32)],
        compiler_params=pltpu.CompilerParams(
            dimension_semantics=("arbitrary", "arbitrary"), vmem_limit_bytes=VMEM_LIMIT),
        name="hgrn2",
    )(proj, lb, g, s0)


def _mlstm_chunk(r0, c, G, qk_ref, v_ref, og_ref, sm_ref, bias_ref, g_ref, y_ref, c_scr, n_scr, m_scr):
    n = G * c
    R = B_HEADS * n
    rows = pl.ds(r0, n)
    rn, cn = _iotas(n)
    seq = (rn // c) == (cn // c)
    pre = sm_ref[rows, :] + bias_ref[...]
    lf = -_softplus(-pre)
    bcum = _mm_hi((seq & (rn >= cn)).astype(F32), lf)
    brev = _mm_hi((seq & (rn < cn)).astype(F32), lf)
    stack = lambda f: jnp.concatenate([f(h) for h in range(B_HEADS)], axis=0)
    q = stack(lambda h: qk_ref[rows, h * 64:(h + 1) * 64]) * (B_QK ** -0.5)
    k = stack(lambda h: qk_ref[rows, 256 + h * 64:256 + (h + 1) * 64])
    v = stack(lambda h: v_ref[rows, h * 128:(h + 1) * 128])
    ic = stack(lambda h: pre[:, h:h + 1])
    bc = stack(lambda h: bcum[:, 4 + h:5 + h])
    gg = stack(lambda h: brev[:, 4 + h:5 + h]) + ic
    m_st = m_scr[:, 0:1]

    def score_terms():
        ri, ci = _iotas(R)
        incl = ((ri // c) == (ci // c)) & (ri >= ci)
        eye = ri == ci
        logd = jnp.where(incl, bc - _col2row(bc, eye) + _col2row(ic, eye), NEG_BIG)
        return incl, logd, jnp.max(logd, -1, keepdims=True), _mm_nt(q, k)

    if G == 1:
        incl, logd, lmax, qk = score_terms()

    groups = range(B_HEADS * G)
    sls = [slice(j * c, (j + 1) * c) for j in groups]
    cms = [c_scr[j] for j in groups]
    njs = [n_scr[j] for j in groups]
    qc = jnp.concatenate([_mm(q[sl], cm) for sl, cm in zip(sls, cms)], axis=0)
    qn = jnp.concatenate([jnp.sum(q[sl] * nj, -1, keepdims=True) for sl, nj in zip(sls, njs)], axis=0)
    w_olds, wks = [], []
    for j in groups:
        hh, g_ = divmod(j, G)
        m = m_scr[j * c:j * c + 1, 0:1]
        bl = bcum[(g_ + 1) * c - 1:(g_ + 1) * c, 4 + hh:5 + hh]
        m_new = jnp.maximum(bl + m, jnp.max(gg[sls[j]], 0, keepdims=True))
        w_olds.append(jnp.exp(bl + m - m_new))
        wks.append(jnp.exp(gg[sls[j]] - m_new))
        m_scr[sls[j], :] = jnp.broadcast_to(m_new, (c, 128))
    for j in groups:
        c_scr[j] = w_olds[j] * cms[j] + _mm_tn(k[sls[j]], v[sls[j]] * wks[j])
        n_scr[j] = w_olds[j] * njs[j] + jnp.sum(wks[j] * k[sls[j]], 0, keepdims=True)

    if G > 1:
        incl, logd, lmax, qk = score_terms()
    m_inter = bc + m_st
    m_t = jnp.maximum(m_inter, lmax)
    w_inter = jnp.exp(m_inter - m_t)
    sc = qk * _masked_exp(incl, logd - m_t)
    num = w_inter * qc + _mm(sc, v)
    den = w_inter * qn + jnp.sum(sc, -1, keepdims=True)
    hh = num / jnp.maximum(jnp.abs(den), jnp.exp(-m_t))
    for h in range(B_HEADS):
        sl = slice(h * 128, (h + 1) * 128)
        y_ref[rows, sl] = _rms(hh[h * n:(h + 1) * n]) * g_ref[:, sl] * jax.nn.sigmoid(og_ref[rows, sl])


def _mlstm_body(qk_ref, v_ref, og_ref, sm_ref, bias_ref, g_ref, c0_ref, n0_ref, m0_ref,
                y_ref, co_ref, no_ref, mo_ref, c_scr, n_scr, m_scr, *, tl, c, G):
    l = pl.program_id(1)

    @pl.when(l == 0)
    def _():
        for h in range(B_HEADS):
            for g in range(G):
                j = h * G + g
                c_scr[j] = c0_ref[g, h]
                n_scr[j] = n0_ref[g, h:h + 1, :]
                m_scr[j * c:(j + 1) * c, :] = jnp.broadcast_to(m0_ref[g, h:h + 1, :], (c, 128))

    args = (qk_ref, v_ref, og_ref, sm_ref, bias_ref, g_ref, y_ref, c_scr, n_scr, m_scr)
    if tl == c:
        _mlstm_chunk(0, c, G, *args)
    else:
        def step(i, carry):
            _mlstm_chunk(pl.multiple_of(i * c, c), c, G, *args)
            return carry
        lax.fori_loop(0, tl // c, step, 0)

    @pl.when(l == pl.num_programs(1) - 1)
    def _():
        for h in range(B_HEADS):
            for g in range(G):
                j = h * G + g
                co_ref[g, h] = c_scr[j]
                no_ref[g, h:h + 1, :] = n_scr[j]
                mo_ref[g, h:h + 1, :] = m_scr[j * c:j * c + 1, 0:1]


def _mlstm(proj, B, L, tl, c, G, bias_row, g, c0, n0, m0, ls):
    assert G == 1 or (L == tl == c and B % G == 0)
    nl = L // tl
    tr = G * tl
    cb = COL_B // 512
    row = lambda b, l: b * nl + l
    return pl.pallas_call(
        functools.partial(_mlstm_body, tl=tl, c=c, G=G),
        grid=(B // G, nl),
        in_specs=[pl.BlockSpec((tr, 512), lambda b, l: (row(b, l), cb)),
                  pl.BlockSpec((tr, 512), lambda b, l: (row(b, l), cb + 1)),
                  pl.BlockSpec((tr, 512), lambda b, l: (row(b, l), cb + 2)),
                  pl.BlockSpec((tr, 128), lambda b, l: (row(b, l), COL_S // 128)),
                  pl.BlockSpec((1, 128), lambda b, l: (0, 0)),
                  pl.BlockSpec((1, 512), lambda b, l: (0, 0)),
                  pl.BlockSpec((None, G, B_HEADS, B_QK, B_VAL), lambda b, l: (ls, b, 0, 0, 0)),
                  pl.BlockSpec((None, G, B_HEADS, B_QK), lambda b, l: (ls, b, 0, 0)),
                  pl.BlockSpec((None, G, B_HEADS, 1), lambda b, l: (ls, b, 0, 0))],
        out_specs=[pl.BlockSpec((tr, 512), lambda b, l: (row(b, l), 0)),
                   pl.BlockSpec((G, B_HEADS, B_QK, B_VAL), lambda b, l: (b, 0, 0, 0)),
                   pl.BlockSpec((G, B_HEADS, B_QK), lambda b, l: (b, 0, 0)),
                   pl.BlockSpec((G, B_HEADS, 1), lambda b, l: (b, 0, 0))],
        out_shape=[jax.ShapeDtypeStruct((B * L, 512), F32),
                   jax.ShapeDtypeStruct((B, B_HEADS, B_QK, B_VAL), F32),
                   jax.ShapeDtypeStruct((B, B_HEADS, B_QK), F32),
                   jax.ShapeDtypeStruct((B, B_HEADS, 1), F32)],
        scratch_shapes=[pltpu.VMEM((B_HEADS * G, B_QK, B_VAL), F32),
                        pltpu.VMEM((B_HEADS * G, 1, B_QK), F32),
                        pltpu.VMEM((B_HEADS * G * c, 128), F32)],
        compiler_params=pltpu.CompilerParams(
            dimension_semantics=("arbitrary", "arbitrary"), vmem_limit_bytes=VMEM_LIMIT),
        name="mlstm",
    )(proj, proj, proj, proj, bias_row, g, c0, n0, m0)


def _unit_lower_inverse_m1(mats, c, ri, ci):
    if c <= SUB:
        ads, aoffs = mats, None
    else:
        same = (ri // SUB) == (ci // SUB)
        ads = [jnp.where(same, a, 0.0) for a in mats]
        aoffs = [a - ad for a, ad in zip(mats, ads)]
    ts = [-ad for ad in ads]
    ps = ads
    n = 2
    while n < min(c, SUB):
        ps = [_mm(p, p) for p in ps]
        ts = [t + p + _mm(t, p) for t, p in zip(ts, ps)]
        n *= 2
    if aoffs is None:
        return ts
    assert c // SUB == 4
    nns = [aoff + _mm(t, aoff) for t, aoff in zip(ts, aoffs)]
    n2s = [_mm(nn, nn) for nn in nns]
    xs = [t + n2 + _mm(n2, t) for t, n2 in zip(ts, n2s)]
    return [x - nn - _mm(nn, x) for x, nn in zip(xs, nns)]


def _gdn_chunks(r0s, c, G, cv_scr, x_ref, sm_ref, arow_ref, dtrow_ref, g_ref, y_ref, s_scr):
    pre = [_gdn_chunk_pre(r0, c, G, cv_scr, sm_ref, arow_ref, dtrow_ref) for r0 in r0s]
    ri, ci = _iotas(C_HEADS * G * c)
    invs = _unit_lower_inverse_m1([p[0] for p in pre], c, ri, ci)
    sols = [p[1] + _mm(u, p[1]) for p, u in zip(pre, invs)]
    for r0, p, sol in zip(r0s, pre, sols):
        _gdn_chunk_post(r0, c, G, p, sol, x_ref, g_ref, y_ref, s_scr)


def _gdn_chunk_pre(r0, c, G, cv_scr, sm_ref, arow_ref, dtrow_ref):
    n = G * c
    R = C_HEADS * n
    rows = pl.ds(r0, n)
    rn, cn = _iotas(n)
    seq = (rn // c) == (cn // c)
    sm = sm_ref[rows, :]
    beta_all = jax.nn.sigmoid(sm)
    g_all = arow_ref[...] * _softplus(sm + dtrow_ref[...])
    bcum = _mm_hi((seq & (rn >= cn)).astype(F32), g_all)
    brev = _mm_hi((seq & (rn < cn)).astype(F32), g_all)
    stack = lambda f: jnp.concatenate([f(h) for h in range(C_HEADS)], axis=0)
    q = stack(lambda h: cv_scr[rows, h * 128:(h + 1) * 128])
    k = stack(lambda h: cv_scr[rows, 512 + h * 128:512 + (h + 1) * 128])
    v = stack(lambda h: cv_scr[rows, 1024 + h * 128:1024 + (h + 1) * 128])
    beta = stack(lambda h: beta_all[:, 8 + h:9 + h])
    bc = stack(lambda h: bcum[:, 12 + h:13 + h])
    kdec = jnp.exp(stack(lambda h: brev[:, 12 + h:13 + h]))
    q = q * lax.rsqrt(jnp.sum(q * q, -1, keepdims=True) + NORM_EPS) * (C_KEY ** -0.5)
    k = k * lax.rsqrt(jnp.sum(k * k, -1, keepdims=True) + NORM_EPS)

    ri, ci = _iotas(R)
    grp = (ri // c) == (ci // c)
    incl = grp & (ri >= ci)
    strict = grp & (ri > ci)
    diff = bc - _col2row(bc, ri == ci)
    dec = jnp.exp(jnp.where(incl, diff, 0.0))
    a = beta * _mm_nt(k, k) * jnp.where(strict, dec, 0.0)
    ebc = jnp.exp(bc)
    rhs = jnp.concatenate([v * beta, k * (beta * ebc)], axis=-1)
    att = _mm_nt(q, k) * jnp.where(incl, dec, 0.0)
    return a, rhs, att, q * ebc, k * kdec, bcum


def _gdn_chunk_post(r0, c, G, pre, sol, x_ref, g_ref, y_ref, s_scr):
    _, _, att, qe, kd, bcum = pre
    n = G * c
    rows = pl.ds(r0, n)
    groups = range(C_HEADS * G)
    sls = [slice(j * c, (j + 1) * c) for j in groups]
    s_old = [s_scr[j] for j in groups]
    v_new = [sol[sl, :C_VAL] - _mm(sol[sl, C_VAL:], s) for sl, s in zip(sls, s_old)]
    o_inter = [_mm(qe[sl], s) for sl, s in zip(sls, s_old)]
    for j in groups:
        hh, gg = divmod(j, G)
        bl = bcum[(gg + 1) * c - 1:(gg + 1) * c, 12 + hh:13 + hh]
        s_scr[j] = jnp.exp(bl) * s_old[j] + _mm_tn(kd[sls[j]], v_new[j])
    o = jnp.concatenate(o_inter, axis=0) + _mm(att, jnp.concatenate(v_new, axis=0))
    cg = x_ref[rows, CONV_CH:2048]
    for h in range(C_HEADS):
        sl = slice(h * 128, (h + 1) * 128)
        y_ref[rows, sl] = _rms(o[h * n:(h + 1) * n]) * g_ref[:, sl] * _silu(cg[:, sl])


def _gdn_body(x_ref, sm_ref, cw_ref, arow_ref, dtrow_ref, g_ref, s0_ref, cs_ref,
              y_ref, so_ref, cso_ref, s_scr, xb_scr, cv_scr, *, tl, c, G):
    l = pl.program_id(1)
    args = (cv_scr, x_ref, sm_ref, arow_ref, dtrow_ref, g_ref, y_ref, s_scr)

    @pl.when(l == 0)
    def _():
        for h in range(C_HEADS):
            for g in range(G):
                s_scr[h * G + g] = s0_ref[g, h]

    if G > 1:
        for g in range(G):
            xb_scr[0:8, :] = cs_ref[g]
            xb_scr[8:8 + c, :] = x_ref[g * c:(g + 1) * c, 0:CONV_CH]
            conv = xb_scr[5:5 + c, :] * cw_ref[0:1, :]
            for j in range(1, CONV_W):
                conv = conv + xb_scr[5 + j:5 + j + c, :] * cw_ref[j:j + 1, :]
            cv_scr[g * c:(g + 1) * c, :] = _silu(conv)
            cso_ref[g] = xb_scr[c + 5:c + 8, :]
        _gdn_chunks([0], c, G, *args)
    else:
        @pl.when(l == 0)
        def _():
            xb_scr[0:8, :] = cs_ref[0]

        xb_scr[8:8 + tl, :] = x_ref[:, 0:CONV_CH]
        xb = xb_scr[...]
        conv = xb[8:] * cw_ref[CONV_W - 1:CONV_W, :]
        for j in range(1, CONV_W):
            conv = conv + pltpu.roll(xb, j, 0)[8:] * cw_ref[CONV_W - 1 - j:CONV_W - j, :]
        cv_scr[...] = _silu(conv)

        _gdn_chunks([i * c for i in range(tl // c)], c, 1, *args)

        @pl.when(l == pl.num_programs(1) - 1)
        def _():
            cso_ref[0] = xb_scr[tl + 5:tl + 8, :]

        xb_scr[0:8, :] = xb_scr[tl:tl + 8, :]

    @pl.when(l == pl.num_programs(1) - 1)
    def _():
        for h in range(C_HEADS):
            for g in range(G):
                so_ref[g, h] = s_scr[h * G + g]


def _gdn(proj, B, L, tl, c, G, conv_w, arow, dtrow, g, s0, cs_pad, ls):
    assert G == 1 or (L == tl == c and B % G == 0)
    nl = L // tl
    tr = G * tl
    row = lambda b, l: b * nl + l
    return pl.pallas_call(
        functools.partial(_gdn_body, tl=tl, c=c, G=G),
        grid=(B // G, nl),
        in_specs=[pl.BlockSpec((tr, 2048), lambda b, l: (row(b, l), COL_C // 2048)),
                  pl.BlockSpec((tr, 128), lambda b, l: (row(b, l), COL_S // 128)),
                  pl.BlockSpec((CONV_W, CONV_CH), lambda b, l: (0, 0)),
                  pl.BlockSpec((1, 128), lambda b, l: (0, 0)),
                  pl.BlockSpec((1, 128), lambda b, l: (0, 0)),
                  pl.BlockSpec((1, 512), lambda b, l: (0, 0)),
                  pl.BlockSpec((None, G, C_HEADS, C_KEY, C_VAL), lambda b, l: (ls, b, 0, 0, 0)),
                  pl.BlockSpec((None, G, 8, CONV_CH), lambda b, l: (ls, b, 0, 0))],
        out_specs=[pl.BlockSpec((tr, 512), lambda b, l: (row(b, l), 0)),
                   pl.BlockSpec((G, C_HEADS, C_KEY, C_VAL), lambda b, l: (b, 0, 0, 0)),
                   pl.BlockSpec((G, CONV_W - 1, CONV_CH), lambda b, l: (b, 0, 0))],
        out_shape=[jax.ShapeDtypeStruct((B * L, 512), F32),
                   jax.ShapeDtypeStruct((B, C_HEADS, C_KEY, C_VAL), F32),
                   jax.ShapeDtypeStruct((B, CONV_W - 1, CONV_CH), F32)],
        scratch_shapes=[pltpu.VMEM((C_HEADS * G, C_KEY, C_VAL), F32),
                        pltpu.VMEM((tl + 8, CONV_CH), F32),
                        pltpu.VMEM((tr, CONV_CH), F32)],
        compiler_params=pltpu.CompilerParams(
            dimension_semantics=("arbitrary", "arbitrary"), vmem_limit_bytes=VMEM_LIMIT),
        name="gdn",
    )(proj, proj, conv_w, arow, dtrow, g, s0, cs_pad)


def _pad_row(vals, start):
    return jnp.zeros((1, 128), F32).at[0, start:start + vals.shape[0]].set(vals.astype(F32))


def _layer(h, B, L, tl, c, G, st, ls, w, lw, l):
    st_a, st_c, st_n, st_m, st_g, st_conv = st
    proj = _proj(h, w["w_in"], l)
    ya, na = _hgrn(proj, B, L, tl, c, G, lw["lb"], lw["a_g"], st_a, ls)
    yb, nc, nn, nm = _mlstm(proj, B, L, tl, c, G, lw["b_bias"], lw["b_g"], st_c, st_n, st_m, ls)
    yc, ng, nv = _gdn(proj, B, L, tl, c, G, lw["conv_w"], lw["c_arow"], lw["c_dtrow"], lw["c_g"], st_g, st_conv, ls)
    h = _post(ya, yb, yc, proj, h, w["wb"], w["wo"], w["ln1_g"], w["ln1_b"], w["wfi"], w["wfo"],
              w["ln2_g"], w["ln2_b"], l)
    return h, (na, jnp.swapaxes(nc, -1, -2), nn, nm[..., 0], ng, nv)


def kernel(x_prompt, x_sample, state_hgrn, state_mlstm_c, state_mlstm_n, state_mlstm_m, state_gdn,
           state_gdn_conv, w_in, lb_logits, a_norm_g, b_mi, b_mf, b_norm_g, conv_w, a_log, dt_bias,
           c_norm_g, w_branch, w_out, ln1_g, ln1_b, w_ffn_in, w_ffn_out, ln2_g, ln2_b):
    Bp, Lp, _ = x_prompt.shape
    Bs, Ls, _ = x_sample.shape
    cp, cs = math.gcd(Lp, CHUNK), math.gcd(Ls, CHUNK)
    tlp = min(Lp, 256)

    w = dict(w_in=_reorder_w_in(w_in), wb=w_branch.astype(BF16), wo=w_out.astype(BF16),
             wfi=w_ffn_in.astype(BF16), wfo=w_ffn_out.astype(BF16),
             ln1_g=ln1_g[:, None].astype(F32), ln1_b=ln1_b[:, None].astype(F32),
             ln2_g=ln2_g[:, None].astype(F32), ln2_b=ln2_b[:, None].astype(F32))
    lb_w = jax.nn.softmax(lb_logits.astype(F32), axis=0)
    lb_all = jnp.cumsum(lb_w, axis=0) - lb_w[0]
    hist_pad = ((0, 0), (0, 0), (8 - (CONV_W - 1), 0), (0, 0))

    h_p = x_prompt.reshape(Bp * Lp, D_MODEL)
    h_s = x_sample.reshape(Bs * Ls, D_MODEL)
    st_p = (jnp.zeros((1, Bp, A_HEADS, A_KEY, A_VAL), F32), jnp.zeros((1, Bp, B_HEADS, B_QK, B_VAL), F32),
            jnp.zeros((1, Bp, B_HEADS, B_QK), F32), jnp.zeros((1, Bp, B_HEADS, 1), F32),
            jnp.zeros((1, Bp, C_HEADS, C_KEY, C_VAL), F32), jnp.zeros((1, Bp, 8, CONV_CH), F32))
    st_s = (state_hgrn.astype(F32), jnp.swapaxes(state_mlstm_c.astype(F32), -1, -2), state_mlstm_n.astype(F32),
            state_mlstm_m.astype(F32)[..., None], state_gdn.astype(F32),
            jnp.pad(state_gdn_conv.astype(F32), hist_pad))
    new_p, new_s = [], []
    for l in range(DEPTH):
        lw = dict(lb=lb_all[l][None], a_g=a_norm_g[l][None].astype(F32),
                  b_bias=_pad_row(jnp.concatenate([b_mi[l], b_mf[l]]), 0), b_g=b_norm_g[l][None].astype(F32),
                  conv_w=conv_w[l].astype(F32), c_arow=_pad_row(-jnp.exp(a_log[l].astype(F32)), 12),
                  c_dtrow=_pad_row(dt_bias[l], 12), c_g=c_norm_g[l][None].astype(F32))
        h_p, ns_p = _layer(h_p, Bp, Lp, tlp, cp, 1, st_p, 0, w, lw, l)
        h_s, ns_s = _layer(h_s, Bs, Ls, Ls, cs, SAMPLE_G, st_s, l, w, lw, l)
        new_p.append(ns_p)
        new_s.append(ns_s)

    outs = [h_p.reshape(Bp, Lp, D_MODEL), h_s.reshape(Bs, Ls, D_MODEL)]
    for i, ref in enumerate((state_hgrn, state_mlstm_c, state_mlstm_n, state_mlstm_m, state_gdn,
                             state_gdn_conv)):
        outs.append(jnp.stack([ns[i] for ns in new_p]).astype(ref.dtype))
        outs.append(jnp.stack([ns[i] for ns in new_s]).astype(ref.dtype))
    return tuple(outs)
```

```python
import functools
import math

import jax
import jax.numpy as jnp
from jax import lax
from jax.experimental import pallas as pl
from jax.experimental.pallas import tpu as pltpu

F32 = jnp.float32
BF16 = jnp.bfloat16
MXU_DT = BF16

D_MODEL = 1024
DEPTH = 4
A_HEADS, A_KEY, A_VAL = 4, 128, 128
B_HEADS, B_QK, B_VAL = 4, 64, 128
C_HEADS, C_KEY, C_VAL = 4, 128, 128
CONV_W = 4
CONV_CH = C_HEADS * (2 * C_KEY + C_VAL)
N_BRANCH = 3
BRANCH_W = 512
FFN_HIDDEN = 2816
CHUNK = 64
ALPHA = (2 * DEPTH) ** 0.25
LN_EPS = 1e-5
NORM_EPS = 1e-6
NEG_BIG = -1e30
LOG2E = 1.0 / math.log(2.0)

COL_A = 0
COL_C = 2048
COL_MG = 4096
COL_B = 7168
COL_S = 8704
N_PROJ = 8832
PROJ_TN = N_PROJ // 3

VMEM_LIMIT = 56 * 1024 * 1024


def _reorder_w_in(w_in):
    a = w_in[..., 0:2048]
    b = w_in[..., 2048:3584]
    s1 = w_in[..., 3584:3592]
    c = w_in[..., 3592:5640]
    s2 = w_in[..., 5640:5648]
    mg = w_in[..., 5648:8720]
    pad = jnp.zeros(w_in.shape[:-1] + (N_PROJ - COL_S - 16,), w_in.dtype)
    return jnp.concatenate([a, c, mg, b, s1, s2, pad], axis=-1).astype(BF16)


def _proj_body(x_ref, w_ref, o_ref):
    o_ref[...] = jnp.dot(x_ref[...].astype(BF16), w_ref[...], preferred_element_type=F32)


def _proj(h, w, l):
    T = h.shape[0]
    tm = min(1024, T)
    return pl.pallas_call(
        _proj_body,
        grid=(T // tm, N_PROJ // PROJ_TN),
        in_specs=[pl.BlockSpec((tm, D_MODEL), lambda i, j: (i, 0)),
                  pl.BlockSpec((None, D_MODEL, PROJ_TN), lambda i, j: (l, 0, j))],
        out_specs=pl.BlockSpec((tm, PROJ_TN), lambda i, j: (i, j)),
        out_shape=jax.ShapeDtypeStruct((T, N_PROJ), F32),
        compiler_params=pltpu.CompilerParams(
            dimension_semantics=("arbitrary", "arbitrary"), vmem_limit_bytes=VMEM_LIMIT),
        name="in_proj",
    )(h, w)


def _ln(x, g, b):
    mu = jnp.mean(x, -1, keepdims=True)
    xc = x - mu
    var = jnp.mean(xc * xc, -1, keepdims=True)
    return xc * lax.rsqrt(var + LN_EPS) * g + b


FFN_HC = 256


def _post_body(ya_ref, yb_ref, yc_ref, mg0_ref, mg1_ref, mg2_ref, h_ref, wb_ref, wo_ref, g1_ref, b1_ref,
               wfi_ref, wfo_ref, g2_ref, b2_ref, o_ref, a_scr):
    mgs = (mg0_ref, mg1_ref, mg2_ref)
    ys = (ya_ref, yb_ref, yc_ref)
    merged = None
    for n in range(N_BRANCH):
        y = ys[n][...].astype(BF16)
        z = jnp.dot(y, wb_ref[n], preferred_element_type=F32)
        t = jax.nn.sigmoid(mgs[n][...]) * z
        merged = t if merged is None else merged + t
    mix = jnp.dot(merged.astype(BF16), wo_ref[...], preferred_element_type=F32)
    h1 = _ln(ALPHA * h_ref[...] + mix, g1_ref[...], b1_ref[...])
    h1b = h1.astype(BF16)
    for j in range(FFN_HIDDEN // FFN_HC):
        gate = jnp.dot(h1b, wfi_ref[:, j * FFN_HC:(j + 1) * FFN_HC], preferred_element_type=F32)
        up = jnp.dot(h1b, wfi_ref[:, FFN_HIDDEN + j * FFN_HC:FFN_HIDDEN + (j + 1) * FFN_HC],
                     preferred_element_type=F32)
        a_scr[:, j * FFN_HC:(j + 1) * FFN_HC] = (gate * jax.nn.sigmoid(gate) * up).astype(BF16)
    ff = jnp.dot(a_scr[...], wfo_ref[...], preferred_element_type=F32)
    o_ref[...] = _ln(ALPHA * h1 + ff, g2_ref[...], b2_ref[...])


def _layer_spec(shape, l):
    nd = len(shape)
    return pl.BlockSpec((None,) + shape, lambda i: (l,) + (0,) * nd, pipeline_mode=pl.Buffered(1))


def _post(ya, yb, yc, proj, h, wb, wo, g1, b1, wfi, wfo, g2, b2, l):
    T = h.shape[0]
    tm = min(256, T)
    mg_blk = COL_MG // D_MODEL
    return pl.pallas_call(
        _post_body,
        grid=(T // tm,),
        in_specs=[pl.BlockSpec((tm, BRANCH_W), lambda i: (i, 0)),
                  pl.BlockSpec((tm, BRANCH_W), lambda i: (i, 0)),
                  pl.BlockSpec((tm, BRANCH_W), lambda i: (i, 0)),
                  pl.BlockSpec((tm, D_MODEL), lambda i: (i, mg_blk)),
                  pl.BlockSpec((tm, D_MODEL), lambda i: (i, mg_blk + 1)),
                  pl.BlockSpec((tm, D_MODEL), lambda i: (i, mg_blk + 2)),
                  pl.BlockSpec((tm, D_MODEL), lambda i: (i, 0)),
                  _layer_spec((N_BRANCH, BRANCH_W, D_MODEL), l),
                  _layer_spec((D_MODEL, D_MODEL), l),
                  _layer_spec((1, D_MODEL), l), _layer_spec((1, D_MODEL), l),
                  _layer_spec((D_MODEL, 2 * FFN_HIDDEN), l),
                  _layer_spec((FFN_HIDDEN, D_MODEL), l),
                  _layer_spec((1, D_MODEL), l), _layer_spec((1, D_MODEL), l)],
        out_specs=pl.BlockSpec((tm, D_MODEL), lambda i: (i, 0)),
        out_shape=jax.ShapeDtypeStruct((T, D_MODEL), F32),
        scratch_shapes=[pltpu.VMEM((tm, FFN_HIDDEN), BF16)],
        compiler_params=pltpu.CompilerParams(
            dimension_semantics=("arbitrary",), vmem_limit_bytes=VMEM_LIMIT),
        name="merge_ffn",
    )(ya, yb, yc, proj, proj, proj, h, wb, wo, g1, b1, wfi, wfo, g2, b2)


SAMPLE_G = 8
SUB = 16
SUB_A = 16
DECAY_SAFE = 60.0
_HI = lax.Precision.HIGHEST


def _mm(a, b):
    return jnp.dot(a.astype(MXU_DT), b.astype(MXU_DT), preferred_element_type=F32)


def _mm_nt(a, b):
    return lax.dot_general(a.astype(MXU_DT), b.astype(MXU_DT), (((1,), (1,)), ((), ())),
                           preferred_element_type=F32)


def _mm_tn(a, b):
    return lax.dot_general(a.astype(MXU_DT), b.astype(MXU_DT), (((0,), (0,)), ((), ())),
                           preferred_element_type=F32)


def _mm_hi(a, b):
    return jnp.dot(a, b, precision=_HI, preferred_element_type=F32)


def _silu(x):
    return x * jax.nn.sigmoid(x)


def _softplus(x):
    return jnp.maximum(x, 0.0) + jnp.log1p(jnp.exp(-jnp.abs(x)))


def _masked_exp(mask, t):
    return jnp.where(mask, jnp.exp(jnp.where(mask, t, 0.0)), 0.0)


def _rms(o):
    return o * lax.rsqrt(jnp.mean(o * o, -1, keepdims=True) + NORM_EPS)


def _col2row(col, eye):
    return jnp.sum(jnp.where(eye, col, 0.0), axis=0, keepdims=True)


def _iotas(c):
    ri = lax.broadcasted_iota(jnp.int32, (c, c), 0)
    ci = lax.broadcasted_iota(jnp.int32, (c, c), 1)
    return ri, ci


def _hgrn_prelude(r0, c, a_ref, lb_ref):
    rows = pl.ds(r0, c)
    ri, ci = _iotas(c)
    tri = (ri >= ci).astype(F32)
    lb = lb_ref[...]
    af = a_ref[rows, 512:1024]
    logf = jnp.log(lb + (1.0 - lb) * jax.nn.sigmoid(af))
    kk = (1.0 - lb) * jax.nn.sigmoid(-af)
    b = _mm_hi(tri, logf)
    return _silu(a_ref[rows, 0:512]), kk, a_ref[rows, 1024:1536], b


def _hgrn_intra_exact(q, kk, v, b, c):
    sb = min(SUB_A, c)
    trow = lax.broadcasted_iota(jnp.int32, (sb, 1), 0)
    b2 = b * LOG2E
    intra = []
    for h in range(A_HEADS):
        sl = slice(h * 128, (h + 1) * 128)
        qh, kh, vh, bh = q[:, sl], kk[:, sl], v[:, sl], b[:, sl]
        blocks = []
        for i in range(c // sb):
            rs = slice(i * sb, (i + 1) * sb)
            qi, bi = qh[rs], bh[rs]
            oi = jnp.zeros((sb, A_VAL), F32)
            if i > 0:
                r = bh[i * sb - 1:i * sb]
                att = _mm_nt(qi * jnp.exp(bi - r), kh[:i * sb] * jnp.exp(r - bh[:i * sb]))
                oi = _mm(att, vh[:i * sb])
            ki, vi, b2i = kh[rs], vh[rs], b2[rs, sl]
            for s in range(sb):
                e = jnp.where(trow >= s, jnp.exp2(b2i - b2i[s:s + 1]), 0.0)
                oi = oi + jnp.sum(e * qi * ki[s:s + 1], axis=-1, keepdims=True) * vi[s:s + 1]
            blocks.append(oi)
        intra.append(blocks[0] if len(blocks) == 1 else jnp.concatenate(blocks, axis=0))
    return intra


def _hgrn_intra_bounded(pres, c):
    sb = min(SUB_A, c)
    trow = lax.broadcasted_iota(jnp.int32, (sb, 1), 0)
    jobs = []
    for q, kk, v, b in pres:
        for h in range(A_HEADS):
            sl = slice(h * 128, (h + 1) * 128)
            for i in range(c // sb):
                rs, nk = slice(i * sb, (i + 1) * sb), (i + 1) * sb
                if i == 0:
                    qd, kd = jnp.exp(b[rs, sl]), jnp.exp(-b[:nk, sl])
                else:
                    r = b[i * sb - 1:i * sb, sl]
                    qd, kd = jnp.exp(b[rs, sl] - r), jnp.exp(r - b[:nk, sl])
                jobs.append((i, nk, q[rs, sl] * qd, kk[:nk, sl] * kd, v[:nk, sl]))
    atts = [_mm_nt(lhs, rhs) for _, _, lhs, rhs, _ in jobs]
    atts = [jnp.where(lax.broadcasted_iota(jnp.int32, (sb, nk), 1) <= i * sb + trow, a, 0.0)
            for (i, nk, _, _, _), a in zip(jobs, atts)]
    outs = [_mm(a, vv) for (_, _, _, _, vv), a in zip(jobs, atts)]
    nb = c // sb
    per_head = [outs[j * nb:(j + 1) * nb] for j in range(len(pres) * A_HEADS)]
    per_head = [o[0] if nb == 1 else jnp.concatenate(o, axis=0) for o in per_head]
    return [per_head[ci * A_HEADS:(ci + 1) * A_HEADS] for ci in range(len(pres))]


def _hgrn_decay_span(b, c):
    sb = min(SUB_A, c)
    ends = [b[(i + 1) * sb - 1:(i + 1) * sb] for i in range(c // sb)]
    span = -ends[0]
    for i in range(1, c // sb):
        span = jnp.maximum(span, ends[i - 1] - ends[i])
    return jnp.max(span)


def _hgrn_state_operands(pre, intra, c):
    q, kk, v, b = pre
    bl = b[c - 1:c]
    return intra, q * jnp.exp(b), kk * jnp.exp(bl - b), v, jnp.exp(bl)


def _hgrn_post(units, c, a_ref, g_ref, y_ref, st_scr):
    heads = [slice(h * 128, (h + 1) * 128) for h in range(A_HEADS)]
    jobs = [(r0, g * A_HEADS + h, pre, sl) for r0, g, pre in units for h, sl in enumerate(heads)]
    sts = [st_scr[j] for _, j, _, _ in jobs]
    outs = [pre[0][(j % A_HEADS)] + _mm_nt(pre[1][:, sl], st) for (_, j, pre, sl), st in zip(jobs, sts)]
    for (_, j, pre, sl), st in zip(jobs, sts):
        _, _, kd, v, ebl = pre
        st_scr[j] = st * ebl[:, sl] + _mm_tn(v[:, sl], kd[:, sl])
    for (r0, _, _, sl), o in zip(jobs, outs):
        rows = pl.ds(r0, c)
        y_ref[rows, sl] = _rms(o) * g_ref[:, sl] * _silu(a_ref[rows, 1536 + sl.start:1536 + sl.stop])


def _hgrn_body(a_ref, lb_ref, g_ref, s0_ref, *refs, c, nchunks, G):
    y_ref, so_ref, st_scr = refs[-3:]
    l = pl.program_id(1)

    @pl.when(l == 0)
    def _():
        for g in range(G):
            for h in range(A_HEADS):
                st_scr[g * A_HEADS + h] = s0_ref[g, h].T

    if G > 1:
        pres = [_hgrn_prelude(g * c, c, a_ref, lb_ref) for g in range(G)]
        units = [(g * c, g, _hgrn_state_operands(p, _hgrn_intra_exact(*p, c), c)) for g, p in enumerate(pres)]
        _hgrn_post(units, c, a_ref, g_ref, y_ref, st_scr)
    else:
        pres = [_hgrn_prelude(i * c, c, a_ref, lb_ref) for i in range(nchunks)]
        span = functools.reduce(jnp.maximum, [_hgrn_decay_span(p[3], c) for p in pres])
        intra = lax.cond(span <= DECAY_SAFE,
                         lambda: _hgrn_intra_bounded(pres, c),
                         lambda: [_hgrn_intra_exact(*p, c) for p in pres])
        for i, p in enumerate(pres):
            _hgrn_post([(i * c, 0, _hgrn_state_operands(p, intra[i], c))], c, a_ref, g_ref, y_ref, st_scr)

    @pl.when(l == pl.num_programs(1) - 1)
    def _():
        for g in range(G):
            for h in range(A_HEADS):
                so_ref[g, h] = st_scr[g * A_HEADS + h].T


def _stacked_state_out(stack, lo, G, tail, n_in):
    nd = len(tail)
    spec = pl.BlockSpec((None, G) + tail, lambda b, l: (lo, b) + (0,) * nd)
    return spec, jax.ShapeDtypeStruct(stack.shape, F32), pl.BlockSpec(memory_space=pl.ANY), {n_in: 1}


def _hgrn(proj, B, L, tl, c, G, lb, g, s0, ls, stack=None, lo=0):
    assert G == 1 or (L == tl == c and B % G == 0)
    nl = L // tl
    tr = G * tl
    in_specs = [pl.BlockSpec((tr, 2048), lambda b, l: (b * nl + l, COL_A // 2048)),
                pl.BlockSpec((1, 512), lambda b, l: (0, 0)),
                pl.BlockSpec((1, 512), lambda b, l: (0, 0)),
                pl.BlockSpec((None, G, A_HEADS, A_KEY, A_VAL), lambda b, l: (ls, b, 0, 0, 0))]
    st_spec = pl.BlockSpec((G, A_HEADS, A_KEY, A_VAL), lambda b, l: (b, 0, 0, 0))
    st_shape = jax.ShapeDtypeStruct((B, A_HEADS, A_KEY, A_VAL), F32)
    args, aliases = [proj, lb, g, s0], {}
    if stack is not None:
        st_spec, st_shape, any_spec, aliases = _stacked_state_out(stack, lo, G, (A_HEADS, A_KEY, A_VAL), 4)
        in_specs.append(any_spec)
        args.append(stack)
    return pl.pallas_call(
        functools.partial(_hgrn_body, c=c, nchunks=tl // c, G=G),
        grid=(B // G, nl),
        in_specs=in_specs,
        out_specs=[pl.BlockSpec((tr, 512), lambda b, l: (b * nl + l, 0)), st_spec],
        out_shape=[jax.ShapeDtypeStruct((B * L, 512), F32), st_shape],
        scratch_shapes=[pltpu.VMEM((G * A_HEADS, A_VAL, A_KEY), F32)],
        input_output_aliases=aliases,
        compiler_params=pltpu.CompilerParams(
            dimension_semantics=("arbitrary", "arbitrary"), vmem_limit_bytes=VMEM_LIMIT),
        name="hgrn2",
    )(*args)


def _mlstm_chunk(r0, c, G, qk_ref, v_ref, og_ref, sm_ref, bias_ref, g_ref, y_ref, c_scr, n_scr, m_scr):
    n = G * c
    R = B_HEADS * n
    rows = pl.ds(r0, n)
    rn, cn = _iotas(n)
    seq = (rn // c) == (cn // c)
    pre = sm_ref[rows, :] + bias_ref[...]
    lf = -_softplus(-pre)
    bcum = _mm_hi((seq & (rn >= cn)).astype(F32), lf)
    brev = _mm_hi((seq & (rn < cn)).astype(F32), lf)
    stack = lambda f: jnp.concatenate([f(h) for h in range(B_HEADS)], axis=0)
    q = stack(lambda h: qk_ref[rows, h * 64:(h + 1) * 64]) * (B_QK ** -0.5)
    k = stack(lambda h: qk_ref[rows, 256 + h * 64:256 + (h + 1) * 64])
    v = stack(lambda h: v_ref[rows, h * 128:(h + 1) * 128])
    ic = stack(lambda h: pre[:, h:h + 1])
    bc = stack(lambda h: bcum[:, 4 + h:5 + h])
    gg = stack(lambda h: brev[:, 4 + h:5 + h]) + ic
    m_st = m_scr[:, 0:1]

    def score_terms():
        ri, ci = _iotas(R)
        incl = ((ri // c) == (ci // c)) & (ri >= ci)
        eye = ri == ci
        logd = jnp.where(incl, bc - _col2row(bc, eye) + _col2row(ic, eye), NEG_BIG)
        return incl, logd, jnp.max(logd, -1, keepdims=True), _mm_nt(q, k)

    if G == 1:
        incl, logd, lmax, qk = score_terms()

    groups = range(B_HEADS * G)
    sls = [slice(j * c, (j + 1) * c) for j in groups]
    cms = [c_scr[j] for j in groups]
    njs = [n_scr[j] for j in groups]
    qc = jnp.concatenate([_mm(q[sl], cm) for sl, cm in zip(sls, cms)], axis=0)
    qn = jnp.concatenate([jnp.sum(q[sl] * nj, -1, keepdims=True) for sl, nj in zip(sls, njs)], axis=0)
    w_olds, wks = [], []
    for j in groups:
        hh, g_ = divmod(j, G)
        m = m_scr[j * c:j * c + 1, 0:1]
        bl = bcum[(g_ + 1) * c - 1:(g_ + 1) * c, 4 + hh:5 + hh]
        m_new = jnp.maximum(bl + m, jnp.max(gg[sls[j]], 0, keepdims=True))
        w_olds.append(jnp.exp(bl + m - m_new))
        wks.append(jnp.exp(gg[sls[j]] - m_new))
        m_scr[sls[j], :] = jnp.broadcast_to(m_new, (c, 128))
    for j in groups:
        c_scr[j] = w_olds[j] * cms[j] + _mm_tn(k[sls[j]], v[sls[j]] * wks[j])
        n_scr[j] = w_olds[j] * njs[j] + jnp.sum(wks[j] * k[sls[j]], 0, keepdims=True)

    if G > 1:
        incl, logd, lmax, qk = score_terms()
    m_inter = bc + m_st
    m_t = jnp.maximum(m_inter, lmax)
    w_inter = jnp.exp(m_inter - m_t)
    sc = qk * _masked_exp(incl, logd - m_t)
    num = w_inter * qc + _mm(sc, v)
    den = w_inter * qn + jnp.sum(sc, -1, keepdims=True)
    hh = num / jnp.maximum(jnp.abs(den), jnp.exp(-m_t))
    for h in range(B_HEADS):
        sl = slice(h * 128, (h + 1) * 128)
        y_ref[rows, sl] = _rms(hh[h * n:(h + 1) * n]) * g_ref[:, sl] * jax.nn.sigmoid(og_ref[rows, sl])


def _mlstm_body(qk_ref, v_ref, og_ref, sm_ref, bias_ref, g_ref, c0_ref, n0_ref, m0_ref,
                y_ref, co_ref, no_ref, mo_ref, c_scr, n_scr, m_scr, *, tl, c, G):
    l = pl.program_id(1)

    @pl.when(l == 0)
    def _():
        for h in range(B_HEADS):
            for g in range(G):
                j = h * G + g
                c_scr[j] = c0_ref[g, h]
                n_scr[j] = n0_ref[g, h:h + 1, :]
                m_scr[j * c:(j + 1) * c, :] = jnp.broadcast_to(m0_ref[g, h:h + 1, :], (c, 128))

    args = (qk_ref, v_ref, og_ref, sm_ref, bias_ref, g_ref, y_ref, c_scr, n_scr, m_scr)
    if tl == c:
        _mlstm_chunk(0, c, G, *args)
    else:
        def step(i, carry):
            _mlstm_chunk(pl.multiple_of(i * c, c), c, G, *args)
            return carry
        lax.fori_loop(0, tl // c, step, 0)

    @pl.when(l == pl.num_programs(1) - 1)
    def _():
        for h in range(B_HEADS):
            for g in range(G):
                j = h * G + g
                co_ref[g, h] = c_scr[j]
                no_ref[g, h:h + 1, :] = n_scr[j]
                mo_ref[g, h:h + 1, :] = m_scr[j * c:j * c + 1, 0:1]


def _mlstm(proj, B, L, tl, c, G, bias_row, g, c0, n0, m0, ls):
    assert G == 1 or (L == tl == c and B % G == 0)
    nl = L // tl
    tr = G * tl
    cb = COL_B // 512
    row = lambda b, l: b * nl + l
    return pl.pallas_call(
        functools.partial(_mlstm_body, tl=tl, c=c, G=G),
        grid=(B // G, nl),
        in_specs=[pl.BlockSpec((tr, 512), lambda b, l: (row(b, l), cb)),
                  pl.BlockSpec((tr, 512), lambda b, l: (row(b, l), cb + 1)),
                  pl.BlockSpec((tr, 512), lambda b, l: (row(b, l), cb + 2)),
                  pl.BlockSpec((tr, 128), lambda b, l: (row(b, l), COL_S // 128)),
                  pl.BlockSpec((1, 128), lambda b, l: (0, 0)),
                  pl.BlockSpec((1, 512), lambda b, l: (0, 0)),
                  pl.BlockSpec((None, G, B_HEADS, B_QK, B_VAL), lambda b, l: (ls, b, 0, 0, 0)),
                  pl.BlockSpec((None, G, B_HEADS, B_QK), lambda b, l: (ls, b, 0, 0)),
                  pl.BlockSpec((None, G, B_HEADS, 1), lambda b, l: (ls, b, 0, 0))],
        out_specs=[pl.BlockSpec((tr, 512), lambda b, l: (row(b, l), 0)),
                   pl.BlockSpec((G, B_HEADS, B_QK, B_VAL), lambda b, l: (b, 0, 0, 0)),
                   pl.BlockSpec((G, B_HEADS, B_QK), lambda b, l: (b, 0, 0)),
                   pl.BlockSpec((G, B_HEADS, 1), lambda b, l: (b, 0, 0))],
        out_shape=[jax.ShapeDtypeStruct((B * L, 512), F32),
                   jax.ShapeDtypeStruct((B, B_HEADS, B_QK, B_VAL), F32),
                   jax.ShapeDtypeStruct((B, B_HEADS, B_QK), F32),
                   jax.ShapeDtypeStruct((B, B_HEADS, 1), F32)],
        scratch_shapes=[pltpu.VMEM((B_HEADS * G, B_QK, B_VAL), F32),
                        pltpu.VMEM((B_HEADS * G, 1, B_QK), F32),
                        pltpu.VMEM((B_HEADS * G * c, 128), F32)],
        compiler_params=pltpu.CompilerParams(
            dimension_semantics=("arbitrary", "arbitrary"), vmem_limit_bytes=VMEM_LIMIT),
        name="mlstm",
    )(proj, proj, proj, proj, bias_row, g, c0, n0, m0)


def _unit_lower_inverse_m1(mats, c, ri, ci):
    if c <= SUB:
        ads, aoffs = mats, None
    else:
        same = (ri // SUB) == (ci // SUB)
        ads = [jnp.where(same, a, 0.0) for a in mats]
        aoffs = [a - ad for a, ad in zip(mats, ads)]
    ts = [-ad for ad in ads]
    ps = ads
    n = 2
    while n < min(c, SUB):
        ps = [_mm(p, p) for p in ps]
        ts = [t + p + _mm(t, p) for t, p in zip(ts, ps)]
        n *= 2
    if aoffs is None:
        return ts
    assert c // SUB == 4
    nns = [aoff + _mm(t, aoff) for t, aoff in zip(ts, aoffs)]
    n2s = [_mm(nn, nn) for nn in nns]
    xs = [t + n2 + _mm(n2, t) for t, n2 in zip(ts, n2s)]
    return [x - nn - _mm(nn, x) for x, nn in zip(xs, nns)]


def _gdn_chunks(r0s, c, G, cv_scr, x_ref, sm_ref, arow_ref, dtrow_ref, g_ref, y_ref, s_scr):
    pre = [_gdn_chunk_pre(r0, c, G, cv_scr, sm_ref, arow_ref, dtrow_ref) for r0 in r0s]
    ri, ci = _iotas(C_HEADS * G * c)
    invs = _unit_lower_inverse_m1([p[0] for p in pre], c, ri, ci)
    sols = [p[1] + _mm(u, p[1]) for p, u in zip(pre, invs)]
    for r0, p, sol in zip(r0s, pre, sols):
        _gdn_chunk_post(r0, c, G, p, sol, x_ref, g_ref, y_ref, s_scr)


def _gdn_chunk_pre(r0, c, G, cv_scr, sm_ref, arow_ref, dtrow_ref):
    n = G * c
    R = C_HEADS * n
    rows = pl.ds(r0, n)
    rn, cn = _iotas(n)
    seq = (rn // c) == (cn // c)
    sm = sm_ref[rows, :]
    beta_all = jax.nn.sigmoid(sm)
    g_all = arow_ref[...] * _softplus(sm + dtrow_ref[...])
    bcum = _mm_hi((seq & (rn >= cn)).astype(F32), g_all)
    brev = _mm_hi((seq & (rn < cn)).astype(F32), g_all)
    stack = lambda f: jnp.concatenate([f(h) for h in range(C_HEADS)], axis=0)
    q = stack(lambda h: cv_scr[rows, h * 128:(h + 1) * 128])
    k = stack(lambda h: cv_scr[rows, 512 + h * 128:512 + (h + 1) * 128])
    v = stack(lambda h: cv_scr[rows, 1024 + h * 128:1024 + (h + 1) * 128])
    beta = stack(lambda h: beta_all[:, 8 + h:9 + h])
    bc = stack(lambda h: bcum[:, 12 + h:13 + h])
    kdec = jnp.exp(stack(lambda h: brev[:, 12 + h:13 + h]))
    q = q * lax.rsqrt(jnp.sum(q * q, -1, keepdims=True) + NORM_EPS) * (C_KEY ** -0.5)
    k = k * lax.rsqrt(jnp.sum(k * k, -1, keepdims=True) + NORM_EPS)

    ri, ci = _iotas(R)
    grp = (ri // c) == (ci // c)
    incl = grp & (ri >= ci)
    strict = grp & (ri > ci)
    diff = bc - _col2row(bc, ri == ci)
    dec = jnp.exp(jnp.where(incl, diff, 0.0))
    a = beta * _mm_nt(k, k) * jnp.where(strict, dec, 0.0)
    ebc = jnp.exp(bc)
    rhs = jnp.concatenate([v * beta, k * (beta * ebc)], axis=-1)
    att = _mm_nt(q, k) * jnp.where(incl, dec, 0.0)
    return a, rhs, att, q * ebc, k * kdec, bcum


def _gdn_chunk_post(r0, c, G, pre, sol, x_ref, g_ref, y_ref, s_scr):
    _, _, att, qe, kd, bcum = pre
    n = G * c
    rows = pl.ds(r0, n)
    groups = range(C_HEADS * G)
    sls = [slice(j * c, (j + 1) * c) for j in groups]
    s_old = [s_scr[j] for j in groups]
    v_new = [sol[sl, :C_VAL] - _mm(sol[sl, C_VAL:], s) for sl, s in zip(sls, s_old)]
    o_inter = [_mm(qe[sl], s) for sl, s in zip(sls, s_old)]
    for j in groups:
        hh, gg = divmod(j, G)
        bl = bcum[(gg + 1) * c - 1:(gg + 1) * c, 12 + hh:13 + hh]
        s_scr[j] = jnp.exp(bl) * s_old[j] + _mm_tn(kd[sls[j]], v_new[j])
    o = jnp.concatenate(o_inter, axis=0) + _mm(att, jnp.concatenate(v_new, axis=0))
    cg = x_ref[rows, CONV_CH:2048]
    for h in range(C_HEADS):
        sl = slice(h * 128, (h + 1) * 128)
        y_ref[rows, sl] = _rms(o[h * n:(h + 1) * n]) * g_ref[:, sl] * _silu(cg[:, sl])


def _gdn_body(x_ref, sm_ref, cw_ref, arow_ref, dtrow_ref, g_ref, s0_ref, cs_ref, *refs, tl, c, G):
    y_ref, so_ref, cso_ref, s_scr, xb_scr, cv_scr = refs[-6:]
    l = pl.program_id(1)
    args = (cv_scr, x_ref, sm_ref, arow_ref, dtrow_ref, g_ref, y_ref, s_scr)

    @pl.when(l == 0)
    def _():
        for h in range(C_HEADS):
            for g in range(G):
                s_scr[h * G + g] = s0_ref[g, h]

    if G > 1:
        for g in range(G):
            xb_scr[0:8, :] = cs_ref[g]
            xb_scr[8:8 + c, :] = x_ref[g * c:(g + 1) * c, 0:CONV_CH]
            conv = xb_scr[5:5 + c, :] * cw_ref[0:1, :]
            for j in range(1, CONV_W):
                conv = conv + xb_scr[5 + j:5 + j + c, :] * cw_ref[j:j + 1, :]
            cv_scr[g * c:(g + 1) * c, :] = _silu(conv)
            cso_ref[g] = xb_scr[c + 5:c + 8, :]
        _gdn_chunks([0], c, G, *args)
    else:
        @pl.when(l == 0)
        def _():
            xb_scr[0:8, :] = cs_ref[0]

        xb_scr[8:8 + tl, :] = x_ref[:, 0:CONV_CH]
        xb = xb_scr[...]
        conv = xb[8:] * cw_ref[CONV_W - 1:CONV_W, :]
        for j in range(1, CONV_W):
            conv = conv + pltpu.roll(xb, j, 0)[8:] * cw_ref[CONV_W - 1 - j:CONV_W - j, :]
        cv_scr[...] = _silu(conv)

        _gdn_chunks([i * c for i in range(tl // c)], c, 1, *args)

        @pl.when(l == pl.num_programs(1) - 1)
        def _():
            cso_ref[0] = xb_scr[tl + 5:tl + 8, :]

        xb_scr[0:8, :] = xb_scr[tl:tl + 8, :]

    @pl.when(l == pl.num_programs(1) - 1)
    def _():
        for h in range(C_HEADS):
            for g in range(G):
                so_ref[g, h] = s_scr[h * G + g]


def _gdn(proj, B, L, tl, c, G, conv_w, arow, dtrow, g, s0, cs_pad, ls, stack=None, lo=0):
    assert G == 1 or (L == tl == c and B % G == 0)
    nl = L // tl
    tr = G * tl
    row = lambda b, l: b * nl + l
    in_specs = [pl.BlockSpec((tr, 2048), lambda b, l: (row(b, l), COL_C // 2048)),
                pl.BlockSpec((tr, 128), lambda b, l: (row(b, l), COL_S // 128)),
                pl.BlockSpec((CONV_W, CONV_CH), lambda b, l: (0, 0)),
                pl.BlockSpec((1, 128), lambda b, l: (0, 0)),
                pl.BlockSpec((1, 128), lambda b, l: (0, 0)),
                pl.BlockSpec((1, 512), lambda b, l: (0, 0)),
                pl.BlockSpec((None, G, C_HEADS, C_KEY, C_VAL), lambda b, l: (ls, b, 0, 0, 0)),
                pl.BlockSpec((None, G, 8, CONV_CH), lambda b, l: (ls, b, 0, 0))]
    st_spec = pl.BlockSpec((G, C_HEADS, C_KEY, C_VAL), lambda b, l: (b, 0, 0, 0))
    st_shape = jax.ShapeDtypeStruct((B, C_HEADS, C_KEY, C_VAL), F32)
    args, aliases = [proj, proj, conv_w, arow, dtrow, g, s0, cs_pad], {}
    if stack is not None:
        st_spec, st_shape, any_spec, aliases = _stacked_state_out(stack, lo, G, (C_HEADS, C_KEY, C_VAL), 8)
        in_specs.append(any_spec)
        args.append(stack)
    return pl.pallas_call(
        functools.partial(_gdn_body, tl=tl, c=c, G=G),
        grid=(B // G, nl),
        in_specs=in_specs,
        out_specs=[pl.BlockSpec((tr, 512), lambda b, l: (row(b, l), 0)),
                   st_spec,
                   pl.BlockSpec((G, CONV_W - 1, CONV_CH), lambda b, l: (b, 0, 0))],
        out_shape=[jax.ShapeDtypeStruct((B * L, 512), F32),
                   st_shape,
                   jax.ShapeDtypeStruct((B, CONV_W - 1, CONV_CH), F32)],
        input_output_aliases=aliases,
        scratch_shapes=[pltpu.VMEM((C_HEADS * G, C_KEY, C_VAL), F32),
                        pltpu.VMEM((tl + 8, CONV_CH), F32),
                        pltpu.VMEM((tr, CONV_CH), F32)],
        compiler_params=pltpu.CompilerParams(
            dimension_semantics=("arbitrary", "arbitrary"), vmem_limit_bytes=VMEM_LIMIT),
        name="gdn",
    )(*args)


def _pad_row(vals, start):
    return jnp.zeros((1, 128), F32).at[0, start:start + vals.shape[0]].set(vals.astype(F32))


def _layer(h, B, L, tl, c, G, st, ls, w, lw, l, stk_a=None, stk_g=None):
    st_a, st_c, st_n, st_m, st_g, st_conv = st
    proj = _proj(h, w["w_in"], l)
    ya, na = _hgrn(proj, B, L, tl, c, G, lw["lb"], lw["a_g"], st_a, ls, stk_a, l)
    yb, nc, nn, nm = _mlstm(proj, B, L, tl, c, G, lw["b_bias"], lw["b_g"], st_c, st_n, st_m, ls)
    yc, ng, nv = _gdn(proj, B, L, tl, c, G, lw["conv_w"], lw["c_arow"], lw["c_dtrow"], lw["c_g"], st_g, st_conv, ls,
                      stk_g, l)
    h = _post(ya, yb, yc, proj, h, w["wb"], w["wo"], w["ln1_g"], w["ln1_b"], w["wfi"], w["wfo"],
              w["ln2_g"], w["ln2_b"], l)
    return h, (na, jnp.swapaxes(nc, -1, -2), nn, nm[..., 0], ng, nv)


def kernel(x_prompt, x_sample, state_hgrn, state_mlstm_c, state_mlstm_n, state_mlstm_m, state_gdn,
           state_gdn_conv, w_in, lb_logits, a_norm_g, b_mi, b_mf, b_norm_g, conv_w, a_log, dt_bias,
           c_norm_g, w_branch, w_out, ln1_g, ln1_b, w_ffn_in, w_ffn_out, ln2_g, ln2_b):
    Bp, Lp, _ = x_prompt.shape
    Bs, Ls, _ = x_sample.shape
    cp, cs = math.gcd(Lp, CHUNK), math.gcd(Ls, CHUNK)
    tlp = min(Lp, 256)

    w = dict(w_in=_reorder_w_in(w_in), wb=w_branch.astype(BF16), wo=w_out.astype(BF16),
             wfi=w_ffn_in.astype(BF16), wfo=w_ffn_out.astype(BF16),
             ln1_g=ln1_g[:, None].astype(F32), ln1_b=ln1_b[:, None].astype(F32),
             ln2_g=ln2_g[:, None].astype(F32), ln2_b=ln2_b[:, None].astype(F32))
    lb_w = jax.nn.softmax(lb_logits.astype(F32), axis=0)
    lb_all = jnp.cumsum(lb_w, axis=0) - lb_w[0]
    hist_pad = ((0, 0), (0, 0), (8 - (CONV_W - 1), 0), (0, 0))

    h_p = x_prompt.reshape(Bp * Lp, D_MODEL)
    h_s = x_sample.reshape(Bs * Ls, D_MODEL)
    st_p = (jnp.zeros((1, Bp, A_HEADS, A_KEY, A_VAL), F32), jnp.zeros((1, Bp, B_HEADS, B_QK, B_VAL), F32),
            jnp.zeros((1, Bp, B_HEADS, B_QK), F32), jnp.zeros((1, Bp, B_HEADS, 1), F32),
            jnp.zeros((1, Bp, C_HEADS, C_KEY, C_VAL), F32), jnp.zeros((1, Bp, 8, CONV_CH), F32))
    st_s = (state_hgrn.astype(F32), jnp.swapaxes(state_mlstm_c.astype(F32), -1, -2), state_mlstm_n.astype(F32),
            state_mlstm_m.astype(F32)[..., None], state_gdn.astype(F32),
            jnp.pad(state_gdn_conv.astype(F32), hist_pad))
    stk_a = jnp.zeros((DEPTH, Bs, A_HEADS, A_KEY, A_VAL), F32)
    stk_g = jnp.zeros((DEPTH, Bs, C_HEADS, C_KEY, C_VAL), F32)
    new_p, new_s = [], []
    for l in range(DEPTH):
        lw = dict(lb=lb_all[l][None], a_g=a_norm_g[l][None].astype(F32),
                  b_bias=_pad_row(jnp.concatenate([b_mi[l], b_mf[l]]), 0), b_g=b_norm_g[l][None].astype(F32),
                  conv_w=conv_w[l].astype(F32), c_arow=_pad_row(-jnp.exp(a_log[l].astype(F32)), 12),
                  c_dtrow=_pad_row(dt_bias[l], 12), c_g=c_norm_g[l][None].astype(F32))
        h_p, ns_p = _layer(h_p, Bp, Lp, tlp, cp, 1, st_p, 0, w, lw, l)
        h_s, ns_s = _layer(h_s, Bs, Ls, Ls, cs, SAMPLE_G, st_s, l, w, lw, l, stk_a, stk_g)
        stk_a, stk_g = ns_s[0], ns_s[4]
        new_p.append(ns_p)
        new_s.append(ns_s)

    outs = [h_p.reshape(Bp, Lp, D_MODEL), h_s.reshape(Bs, Ls, D_MODEL)]
    for i, ref in enumerate((state_hgrn, state_mlstm_c, state_mlstm_n, state_mlstm_m, state_gdn,
                             state_gdn_conv)):
        outs.append(jnp.stack([ns[i] for ns in new_p]).astype(ref.dtype))
        stacked_s = {0: stk_a, 4: stk_g}.get(i)
        if stacked_s is None:
            stacked_s = jnp.stack([ns[i] for ns in new_s])
        outs.append(stacked_s.astype(ref.dtype))
    return tuple(outs)
```
